```python
import math
import jax, jax.numpy as jnp
from jax import lax
import numpy as np

D_MODEL = 2048
BATCH = 1
SEQ = 16384
DEPTH = 1
DEC_BATCH = 16
DEC_SEQ = 16
PAST_LEN = 2048

CHUNK = 64
Q_BLOCK = 128
N_HEADS = 8
QK_HALF = 64
V_HEAD = 2 * QK_HALF
ATTN_QK_WIDTH = N_HEADS * 2 * QK_HALF
ATTN_V_WIDTH = N_HEADS * V_HEAD
POOL_WINDOWS = (2, 4, 8, 16)
POOL_GROUPS = len(POOL_WINDOWS)
POOL_WIDTH = D_MODEL // 2
POOL_GROUP_DIM = POOL_WIDTH // POOL_GROUPS
POOL_HIST = max(POOL_WINDOWS) - 1
IN_COLS = 2 * ATTN_QK_WIDTH + ATTN_V_WIDTH + POOL_WIDTH + 2 * D_MODEL
CONV_WIDTH = 3
D_FF = 5504
EPS = 1e-6
NEG_INF = -1e30

kernel_name = "hybrid_diffattn_pool_convglu_stream_step"


def rms_norm(x, g):
    xf = x.astype(jnp.float32)
    y = xf * lax.rsqrt(jnp.mean(xf * xf, axis=-1, keepdims=True) + EPS)
    return (y * g.astype(jnp.float32)).astype(x.dtype)


def diff_attn(q, k, v, q_pos, k_pos, lam):
    s = jnp.einsum('bqhcd,bkhcd->bhcqk', q.astype(jnp.float32), k.astype(jnp.float32)) * (QK_HALF ** -0.5)
    mask = (k_pos[None, :] // CHUNK) <= (q_pos[:, None] // CHUNK)
    s = jnp.where(mask, s, NEG_INF)
    p = jax.nn.softmax(s, axis=-1)
    a = p[:, :, 0] - lam * p[:, :, 1]
    return jnp.einsum('bhqk,bkhd->bqhd', a, v.astype(jnp.float32))


def diff_attn_blocked(q, k, v, lam):
    B, S = q.shape[0], q.shape[1]
    nb = S // Q_BLOCK
    qb = q.reshape(B, nb, Q_BLOCK, N_HEADS, 2, QK_HALF).transpose(1, 0, 2, 3, 4, 5)
    qpos = jnp.arange(S).reshape(nb, Q_BLOCK)
    kpos = jnp.arange(S)

    def one(args):
        qblk, qp = args
        return diff_attn(qblk, k, v, qp, kpos, lam)

    o = lax.map(one, (qb, qpos))
    return o.transpose(1, 0, 2, 3, 4).reshape(B, S, N_HEADS, V_HEAD)


def pool_mix(u_ext, start, w_pool, pool_scale):
    B, T_ext = u_ext.shape[0], u_ext.shape[1]
    T = T_ext - POOL_HIST
    uf = u_ext.astype(jnp.float32).reshape(B, T_ext, POOL_GROUPS, POOL_GROUP_DIM)
    cs = jnp.concatenate([jnp.zeros((B, 1, POOL_GROUPS, POOL_GROUP_DIM), jnp.float32),
                          jnp.cumsum(uf, axis=1)], axis=1)
    cur = uf[:, POOL_HIST:]
    pos = start + jnp.arange(T)
    outs = []
    for g, w in enumerate(POOL_WINDOWS):
        win = cs[:, POOL_HIST + 1:POOL_HIST + 1 + T, g] - cs[:, POOL_HIST + 1 - w:POOL_HIST + 1 - w + T, g]
        cnt = jnp.minimum(w, pos + 1).astype(jnp.float32)[None, :, None]
        outs.append(win / cnt - cur[:, :, g])
    p = jnp.stack(outs, axis=2)
    y = jnp.einsum('btgc,gcd->btgd', p, w_pool.astype(jnp.float32)).reshape(B, T, POOL_WIDTH)
    return (y * pool_scale.astype(jnp.float32)).astype(u_ext.dtype)


def layer(x, start, k_hist, v_hist, pool_hist, conv_hist, lam_init,
          attn_norm_g, w_in, q_norm_g, k_norm_g, lambda_q1, lambda_k1, lambda_q2, lambda_k2,
          subln_g, w_pool, pool_scale, w_up_attn, w_up_pool, w_out,
          ffn_norm_g, w_ffn_in, conv_w, conv_b, w_ffn_out):
    B, T = x.shape[0], x.shape[1]
    h = rms_norm(x, attn_norm_g)
    proj = h @ w_in
    o0 = ATTN_QK_WIDTH
    o1 = 2 * ATTN_QK_WIDTH
    o2 = o1 + ATTN_V_WIDTH
    o3 = o2 + POOL_WIDTH
    o4 = o3 + D_MODEL
    q = rms_norm(proj[..., :o0].reshape(B, T, N_HEADS, 2, QK_HALF), q_norm_g)
    k = rms_norm(proj[..., o0:o1].reshape(B, T, N_HEADS, 2, QK_HALF), k_norm_g)
    v = proj[..., o1:o2].reshape(B, T, N_HEADS, V_HEAD)
    u = proj[..., o2:o3]
    g_a = proj[..., o3:o4]
    g_b = proj[..., o4:]

    lam = (jnp.exp(jnp.sum(lambda_q1.astype(jnp.float32) * lambda_k1.astype(jnp.float32)))
           - jnp.exp(jnp.sum(lambda_q2.astype(jnp.float32) * lambda_k2.astype(jnp.float32)))
           + lam_init)
    if k_hist is None:
        o = diff_attn_blocked(q, k, v, lam)
    else:
        P = k_hist.shape[1]
        keys = jnp.concatenate([k_hist.reshape(B, P, N_HEADS, 2, QK_HALF).astype(k.dtype), k], axis=1)
        vals = jnp.concatenate([v_hist.astype(v.dtype), v], axis=1)
        o = diff_attn(q, keys, vals, start + jnp.arange(T), jnp.arange(P + T), lam)
    o = o * lax.rsqrt(jnp.mean(o * o, axis=-1, keepdims=True) + EPS)
    o = (o * subln_g.astype(jnp.float32) * (1.0 - lam_init)).astype(x.dtype).reshape(B, T, ATTN_V_WIDTH)

    u_ext = jnp.concatenate([pool_hist.astype(u.dtype), u], axis=1)
    pb = pool_mix(u_ext, start, w_pool, pool_scale)

    m = jax.nn.sigmoid(g_a) * (o @ w_up_attn) + jax.nn.sigmoid(g_b) * (pb @ w_up_pool)
    x1 = x + m @ w_out

    h2 = rms_norm(x1, ffn_norm_g)
    gv = h2 @ w_ffn_in
    z, val = gv[..., :D_FF], gv[..., D_FF:]
    z_ext = jnp.concatenate([conv_hist.astype(z.dtype), z], axis=1)
    zc = conv_b + z_ext[:, 0:T] * conv_w[0] + z_ext[:, 1:T + 1] * conv_w[1] + z_ext[:, 2:T + 2] * conv_w[2]
    y = x1 + (jax.nn.silu(zc) * val) @ w_ffn_out

    k_rows = k.reshape(B, T, N_HEADS, 2 * QK_HALF)
    return y, k_rows, v, u_ext[:, -POOL_HIST:], z_ext[:, -(CONV_WIDTH - 1):]


def setup_inputs(seed: int = 0) -> dict:
    key = jax.random.key(seed)
    ks = jax.random.split(key, 32)
    f32 = jnp.float32
    nrm = lambda k, shape, s: jax.random.normal(k, shape, f32) * s
    D = D_MODEL
    return {
        "x_prompt": nrm(ks[0], (BATCH, SEQ, D), 1.0),
        "x_sample": nrm(ks[1], (DEC_BATCH, DEC_SEQ, D), 1.0),
        "cache_k": nrm(ks[2], (DEPTH, DEC_BATCH, PAST_LEN, N_HEADS, 2 * QK_HALF), 1.0),
        "cache_v": nrm(ks[3], (DEPTH, DEC_BATCH, PAST_LEN, N_HEADS, V_HEAD), 1.0),
        "state_pool": nrm(ks[4], (DEPTH, DEC_BATCH, POOL_HIST, POOL_WIDTH), 1.0),
        "state_conv": nrm(ks[5], (DEPTH, DEC_BATCH, CONV_WIDTH - 1, D_FF), 1.0),
        "attn_norm_g": 1.0 + nrm(ks[6], (DEPTH, D), 0.05),
        "w_in": nrm(ks[7], (DEPTH, D, IN_COLS), D ** -0.5),
        "q_norm_g": 1.0 + nrm(ks[8], (DEPTH, 2, QK_HALF), 0.05),
        "k_norm_g": 1.0 + nrm(ks[9], (DEPTH, 2, QK_HALF), 0.05),
        "lambda_q1": nrm(ks[10], (DEPTH, QK_HALF), 0.1),
        "lambda_k1": nrm(ks[11], (DEPTH, QK_HALF), 0.1),
        "lambda_q2": nrm(ks[12], (DEPTH, QK_HALF), 0.1),
        "lambda_k2": nrm(ks[13], (DEPTH, QK_HALF), 0.1),
        "subln_g": 1.0 + nrm(ks[14], (DEPTH, V_HEAD), 0.05),
        "w_pool": nrm(ks[15], (DEPTH, POOL_GROUPS, POOL_GROUP_DIM, POOL_GROUP_DIM), POOL_GROUP_DIM ** -0.5),
        "pool_scale": 1.0 + nrm(ks[16], (DEPTH, POOL_WIDTH), 0.05),
        "w_up_attn": nrm(ks[17], (DEPTH, ATTN_V_WIDTH, D), ATTN_V_WIDTH ** -0.5),
        "w_up_pool": nrm(ks[18], (DEPTH, POOL_WIDTH, D), POOL_WIDTH ** -0.5),
        "w_out": nrm(ks[19], (DEPTH, D, D), D ** -0.5),
        "ffn_norm_g": 1.0 + nrm(ks[20], (DEPTH, D), 0.05),
        "w_ffn_in": nrm(ks[21], (DEPTH, D, 2 * D_FF), D ** -0.5),
        "conv_w": nrm(ks[22], (DEPTH, CONV_WIDTH, D_FF), CONV_WIDTH ** -0.5),
        "conv_b": nrm(ks[23], (DEPTH, D_FF), 0.02),
        "w_ffn_out": nrm(ks[24], (DEPTH, D_FF, D), D_FF ** -0.5),
    }


def reference(x_prompt, x_sample, cache_k, cache_v, state_pool, state_conv,
              attn_norm_g, w_in, q_norm_g, k_norm_g, lambda_q1, lambda_k1, lambda_q2, lambda_k2,
              subln_g, w_pool, pool_scale, w_up_attn, w_up_pool, w_out,
              ffn_norm_g, w_ffn_in, conv_w, conv_b, w_ffn_out):
    y_p, y_s = x_prompt, x_sample
    Bp = x_prompt.shape[0]
    past = cache_k.shape[2]
    kp_l, vp_l, pp_l, cp_l, ks_l, vs_l, ps_l, cs_l = [], [], [], [], [], [], [], []
    for l in range(DEPTH):
        lam_init = 0.8 - 0.6 * math.exp(-0.3 * l)
        w = (attn_norm_g[l], w_in[l], q_norm_g[l], k_norm_g[l], lambda_q1[l], lambda_k1[l],
             lambda_q2[l], lambda_k2[l], subln_g[l], w_pool[l], pool_scale[l], w_up_attn[l],
             w_up_pool[l], w_out[l], ffn_norm_g[l], w_ffn_in[l], conv_w[l], conv_b[l], w_ffn_out[l])
        pool0 = jnp.zeros((Bp, POOL_HIST, POOL_WIDTH), y_p.dtype)
        conv0 = jnp.zeros((Bp, CONV_WIDTH - 1, D_FF), y_p.dtype)
        y_p, kp, vp, pp, cp = layer(y_p, 0, None, None, pool0, conv0, lam_init, *w)
        y_s, kk, vv, ps, cs = layer(y_s, past, cache_k[l], cache_v[l], state_pool[l], state_conv[l], lam_init, *w)
        kp_l.append(kp); vp_l.append(vp); pp_l.append(pp); cp_l.append(cp)
        ks_l.append(kk); vs_l.append(vv); ps_l.append(ps); cs_l.append(cs)
    k_prompt = jnp.stack(kp_l)
    v_prompt = jnp.stack(vp_l)
    pool_prompt = jnp.stack(pp_l)
    conv_prompt = jnp.stack(cp_l)
    k_sample = jnp.stack(ks_l)
    v_sample = jnp.stack(vs_l)
    pool_sample = jnp.stack(ps_l)
    conv_sample = jnp.stack(cs_l)
    return (y_p, y_s, k_prompt, v_prompt, pool_prompt, conv_prompt, k_sample, v_sample, pool_sample, conv_sample)
```

```python
import functools
import math

import numpy as np
import jax
import jax.numpy as jnp
from jax import lax
from jax.experimental import pallas as pl
from jax.experimental.pallas import tpu as pltpu

F32 = jnp.float32
BF16 = jnp.bfloat16

EPS = 1e-6
NEG_INF = -1e30
CHUNK = 64
POOL_WINDOWS = (2, 4, 8, 16)
POOL_HIST = max(POOL_WINDOWS) - 1
CONV_WIDTH = 3

LANES = 128
HIST_ROWS = 16
MXU_TILE = 256
VMEM_LIMIT_CAP = 60 * 1024 * 1024
VMEM_TEMP_ALLOWANCE = 12 * 1024 * 1024

HEAD_W = 128
QK_HALF = 64
FF_TILE = 512


def _vmem_limit(block_bytes, scratch_bytes):
    est = 2 * block_bytes + scratch_bytes + VMEM_TEMP_ALLOWANCE
    return int(min(est, VMEM_LIMIT_CAP))


def _nbytes(shape, dtype):
    return int(np.prod(shape)) * jnp.dtype(dtype).itemsize


def _rms_rows(x, g):
    ms = jnp.mean(x * x, axis=-1, keepdims=True)
    return x * lax.rsqrt(ms + EPS) * g


def _dot(a, b):
    return jnp.dot(a, b, preferred_element_type=F32)


def _dot_nt(a, b):
    return lax.dot_general(a, b, (((1,), (1,)), ((), ())), preferred_element_type=F32)


def _inproj_kernel(x_ref, g_ref, w_ref, qg_ref, kg_ref, gd_ref,
                   q_ref, kf_ref, kb_ref, vf_ref, vb_ref, u_ref, sig_ref, h_scr,
                   *, n_heads):
    j = pl.program_id(1)
    tm = x_ref.shape[0]

    @pl.when(j == 0)
    def _():
        h_scr[...] = _rms_rows(x_ref[...], g_ref[...]).astype(BF16)

    y = _dot(h_scr[...], w_ref[...])

    def qk_norm(gain_ref):
        heads = []
        for c in range(n_heads * HEAD_W // MXU_TILE):
            cs = slice(c * MXU_TILE, (c + 1) * MXU_TILE)
            yc = y[:, cs]
            ms = _dot((yc * yc).astype(BF16), gd_ref[...])
            yn = yc * lax.rsqrt(ms + EPS) * gain_ref[:, cs]
            heads += [yn[:, :HEAD_W], yn[:, HEAD_W:]]
        return heads

    @pl.when(j == 0)
    def _():
        lane = lax.broadcasted_iota(jnp.int32, (tm, HEAD_W), 1)
        for h, yh in enumerate(qk_norm(qg_ref)):
            q_ref[h, 0] = jnp.where(lane < QK_HALF, yh, 0.0).astype(BF16)
            q_ref[h, 1] = jnp.where(lane >= QK_HALF, yh, 0.0).astype(BF16)

    @pl.when(j == 1)
    def _():
        for h, yh in enumerate(qk_norm(kg_ref)):
            kf_ref[:, h * HEAD_W:(h + 1) * HEAD_W] = yh
            kb_ref[h] = yh.astype(BF16)

    @pl.when(j == 2)
    def _():
        vf_ref[...] = y
        for h in range(n_heads):
            vb_ref[h] = y[:, h * HEAD_W:(h + 1) * HEAD_W].astype(BF16)

    @pl.when(j == 3)
    def _():
        u_ref[...] = y

    @pl.when(j >= 4)
    def _():
        sig_ref[...] = jax.nn.sigmoid(y).astype(BF16)


def _in_proj(x, g, w_bf, qg, kg, gd, *, tm, n_heads):
    m, d = x.shape
    aw = n_heads * HEAD_W
    n_col = w_bf.shape[1] // aw
    gate_tiles = n_col - 4
    out_shape = (
        jax.ShapeDtypeStruct((n_heads, 2, m, HEAD_W), BF16),
        jax.ShapeDtypeStruct((m, aw), F32),
        jax.ShapeDtypeStruct((n_heads, m, HEAD_W), BF16),
        jax.ShapeDtypeStruct((m, aw), F32),
        jax.ShapeDtypeStruct((n_heads, m, HEAD_W), BF16),
        jax.ShapeDtypeStruct((m, aw), F32),
        jax.ShapeDtypeStruct((m, gate_tiles * aw), BF16),
    )
    in_specs = [
        pl.BlockSpec((tm, d), lambda i, j: (i, 0)),
        pl.BlockSpec((1, d), lambda i, j: (0, 0)),
        pl.BlockSpec((d, aw), lambda i, j: (0, j)),
        pl.BlockSpec((1, aw), lambda i, j: (0, 0)),
        pl.BlockSpec((1, aw), lambda i, j: (0, 0)),
        pl.BlockSpec((MXU_TILE, MXU_TILE), lambda i, j: (0, 0)),
    ]
    out_specs = (
        pl.BlockSpec((n_heads, 2, tm, HEAD_W), lambda i, j: (0, 0, i, 0)),
        pl.BlockSpec((tm, aw), lambda i, j: (i, 0)),
        pl.BlockSpec((n_heads, tm, HEAD_W), lambda i, j: (0, i, 0)),
        pl.BlockSpec((tm, aw), lambda i, j: (i, 0)),
        pl.BlockSpec((n_heads, tm, HEAD_W), lambda i, j: (0, i, 0)),
        pl.BlockSpec((tm, aw), lambda i, j: (i, 0)),
        pl.BlockSpec((tm, aw), lambda i, j: (i, jnp.maximum(j - 4, 0))),
    )
    block_bytes = (_nbytes((tm, d), F32) + _nbytes((d, aw), BF16) + 3 * _nbytes((tm, aw), F32)
                   + 5 * _nbytes((tm, aw), BF16))
    return pl.pallas_call(
        functools.partial(_inproj_kernel, n_heads=n_heads),
        grid=(m // tm, n_col),
        in_specs=in_specs,
        out_specs=out_specs,
        out_shape=out_shape,
        scratch_shapes=[pltpu.VMEM((tm, d), BF16)],
        compiler_params=pltpu.CompilerParams(
            dimension_semantics=("arbitrary", "arbitrary"),
            vmem_limit_bytes=_vmem_limit(block_bytes, _nbytes((tm, d), BF16))),
        name="in_proj",
    )(x, g, w_bf, qg, kg, gd)


def _lambda_value(lam_ref, lam_init):
    a = lam_ref[...]
    s1 = jnp.sum(a[0:1] * a[1:2], axis=1, keepdims=True)
    s2 = jnp.sum(a[2:3] * a[3:4], axis=1, keepdims=True)
    return jnp.exp(s1) - jnp.exp(s2) + lam_init


def _sub_ln(o, sg, lam_init):
    o = o * lax.rsqrt(jnp.mean(o * o, axis=-1, keepdims=True) + EPS)
    return o * sg * (1.0 - lam_init)


def _attn_kernel(qi_ref, kj_ref, flag_ref, lam_ref, sg_ref, q_ref, k_ref, v_ref, o_ref,
                 m_scr, l_scr, acc_scr, *, tq, tk, n_heads, lam_init):
    t = pl.program_id(0)
    i = qi_ref[t]
    j = kj_ref[t]
    flags = flag_ref[t]
    is_first = (flags & 1) != 0
    is_last = (flags & 2) != 0
    is_masked = (flags & 4) != 0

    @pl.when(is_first)
    def _():
        m_scr[...] = jnp.full(m_scr.shape, NEG_INF, F32)
        l_scr[...] = jnp.zeros(l_scr.shape, F32)
        acc_scr[...] = jnp.zeros(acc_scr.shape, F32)

    def step(masked):
        if masked:
            qpos = i * tq + lax.broadcasted_iota(jnp.int32, (tq, tk), 0)
            kpos = j * tk + lax.broadcasted_iota(jnp.int32, (tq, tk), 1)
            mask = (kpos // CHUNK) <= (qpos // CHUNK)

        def head(h, carry):
            kh = k_ref[h]
            vh = v_ref[h]
            for c in range(2):
                s = _dot_nt(q_ref[h, c], kh)
                if masked:
                    s = jnp.where(mask, s, NEG_INF)
                m_prev = m_scr[h, c]
                m_next = jnp.maximum(m_prev, jnp.max(s, axis=1, keepdims=True))
                p = jnp.exp(s - jnp.tile(m_next, (1, tk // LANES)))
                alpha = jnp.exp(m_prev - m_next)
                l_scr[h, c] = alpha * l_scr[h, c] + jnp.sum(p, axis=1, keepdims=True)
                m_scr[h, c] = m_next
                acc_scr[h, c] = alpha * acc_scr[h, c] + _dot(p.astype(BF16), vh)
            return carry

        lax.fori_loop(0, n_heads, head, 0)

    @pl.when(jnp.logical_not(is_masked))
    def _():
        step(False)

    @pl.when(is_masked)
    def _():
        step(True)

    @pl.when(is_last)
    def _():
        lam = _lambda_value(lam_ref, lam_init)

        def head(h, carry):
            o = acc_scr[h, 0] / l_scr[h, 0] - lam * (acc_scr[h, 1] / l_scr[h, 1])
            o_ref[h] = _sub_ln(o, sg_ref[...], lam_init).astype(BF16)
            return carry

        lax.fori_loop(0, n_heads, head, 0)


def _attn_schedule(seq, tq, tk):
    qi, kj, flags = [], [], []
    for i in range(seq // tq):
        q_lo, q_hi = i * tq, (i + 1) * tq - 1
        js = [j for j in range(seq // tk) if (j * tk) // CHUNK <= q_hi // CHUNK]
        for n, j in enumerate(js):
            masked = ((j + 1) * tk - 1) // CHUNK > q_lo // CHUNK
            qi.append(i)
            kj.append(j)
            flags.append((1 if n == 0 else 0) | (2 if n == len(js) - 1 else 0) | (4 if masked else 0))
    return (np.asarray(qi, np.int32), np.asarray(kj, np.int32), np.asarray(flags, np.int32))


def _attn_prompt(q, kb, vb, lam_vec, sg, *, tq, tk, lam_init):
    n_heads, _, seq, _ = q.shape
    qi, kj, flags = _attn_schedule(seq, tq, tk)
    stat = (n_heads, 2, tq, LANES)
    grid_spec = pltpu.PrefetchScalarGridSpec(
        num_scalar_prefetch=3,
        grid=(len(qi),),
        in_specs=[
            pl.BlockSpec((4, LANES), lambda t, qi, kj, fl: (0, 0)),
            pl.BlockSpec((1, HEAD_W), lambda t, qi, kj, fl: (0, 0)),
            pl.BlockSpec((n_heads, 2, tq, HEAD_W), lambda t, qi, kj, fl: (0, 0, qi[t], 0)),
            pl.BlockSpec((n_heads, tk, HEAD_W), lambda t, qi, kj, fl: (0, kj[t], 0)),
            pl.BlockSpec((n_heads, tk, HEAD_W), lambda t, qi, kj, fl: (0, kj[t], 0)),
        ],
        out_specs=pl.BlockSpec((n_heads, tq, HEAD_W), lambda t, qi, kj, fl: (0, qi[t], 0)),
        scratch_shapes=[pltpu.VMEM(stat, F32), pltpu.VMEM(stat, F32), pltpu.VMEM(stat, F32)],
    )
    block_bytes = (_nbytes((n_heads, 2, tq, HEAD_W), BF16) + 2 * _nbytes((n_heads, tk, HEAD_W), BF16)
                   + _nbytes((n_heads, tq, HEAD_W), BF16))
    return pl.pallas_call(
        functools.partial(_attn_kernel, tq=tq, tk=tk, n_heads=n_heads, lam_init=lam_init),
        grid_spec=grid_spec,
        out_shape=jax.ShapeDtypeStruct((n_heads, seq, HEAD_W), BF16),
        compiler_params=pltpu.CompilerParams(
            dimension_semantics=("arbitrary",),
            vmem_limit_bytes=_vmem_limit(block_bytes, 3 * _nbytes(stat, F32))),
        name="attn_prompt",
    )(jnp.asarray(qi), jnp.asarray(kj), jnp.asarray(flags), lam_vec, sg, q, kb, vb)


def _attn_dec_kernel(lam_ref, sg_ref, q_ref, kn_ref, vn_ref, ck_ref, cv_ref, o_ref,
                     *, past, t_new, n_heads, lam_init):
    lam = _lambda_value(lam_ref, lam_init)
    pad_rows = LANES - t_new
    n_keys = past + LANES
    qpos = past + lax.broadcasted_iota(jnp.int32, (t_new, n_keys), 0)
    kpos = lax.broadcasted_iota(jnp.int32, (t_new, n_keys), 1)
    mask = (kpos < past + t_new) & ((kpos // CHUNK) <= (qpos // CHUNK))
    zpad = jnp.zeros((pad_rows, HEAD_W), BF16)
    for h in range(n_heads):
        hs = slice(h * HEAD_W, (h + 1) * HEAD_W)
        k_all = jnp.concatenate([ck_ref[:, hs].astype(BF16), kn_ref[h], zpad], axis=0)
        v_all = jnp.concatenate([cv_ref[:, hs].astype(BF16), vn_ref[h], zpad], axis=0)
        comps = []
        for c in range(2):
            s = jnp.where(mask, _dot_nt(q_ref[h, c], k_all), NEG_INF)
            m = jnp.max(s, axis=1, keepdims=True)
            p = jnp.exp(s - m)
            l = jnp.sum(p, axis=1, keepdims=True)
            comps.append(_dot(p.astype(BF16), v_all) / l)
        o = comps[0] - lam * comps[1]
        o_ref[h] = _sub_ln(o, sg_ref[...], lam_init).astype(BF16)


def _attn_sample(q, kb, vb, cache_k2d, cache_v2d, lam_vec, sg, *, t_new, lam_init):
    n_heads = q.shape[0]
    n_seq, past, aw = cache_k2d.shape
    block_bytes = 2 * _nbytes((past, aw), F32) + 5 * _nbytes((n_heads, t_new, HEAD_W), BF16)
    return pl.pallas_call(
        functools.partial(_attn_dec_kernel, past=past, t_new=t_new, n_heads=n_heads, lam_init=lam_init),
        grid=(n_seq,),
        in_specs=[
            pl.BlockSpec((4, LANES), lambda b: (0, 0)),
            pl.BlockSpec((1, HEAD_W), lambda b: (0, 0)),
            pl.BlockSpec((n_heads, 2, t_new, HEAD_W), lambda b: (0, 0, b, 0)),
            pl.BlockSpec((n_heads, t_new, HEAD_W), lambda b: (0, b, 0)),
            pl.BlockSpec((n_heads, t_new, HEAD_W), lambda b: (0, b, 0)),
            pl.BlockSpec((None, past, aw), lambda b: (b, 0, 0)),
            pl.BlockSpec((None, past, aw), lambda b: (b, 0, 0)),
        ],
        out_specs=pl.BlockSpec((n_heads, t_new, HEAD_W), lambda b: (0, b, 0)),
        out_shape=jax.ShapeDtypeStruct((n_heads, n_seq * t_new, HEAD_W), BF16),
        compiler_params=pltpu.CompilerParams(
            dimension_semantics=("arbitrary",),
            vmem_limit_bytes=_vmem_limit(block_bytes, 0)),
        name="attn_sample",
    )(lam_vec, sg, q, kb, vb, cache_k2d, cache_v2d)


def _pool_kernel(u_ref, hist_ref, wp_ref, ps_ref, pb_ref, ext_scr, *, tm, start):
    ti = pl.program_id(1)

    @pl.when(ti == 0)
    def _():
        ext_scr[0:HIST_ROWS] = hist_ref[...]

    @pl.when(ti > 0)
    def _():
        ext_scr[0:HIST_ROWS] = ext_scr[tm:tm + HIST_ROWS]

    ext_scr[HIST_ROWS:HIST_ROWS + tm] = u_ref[...]
    pos = start + ti * tm + lax.broadcasted_iota(jnp.int32, (tm, 1), 0)
    gw = wp_ref.shape[1]
    for g, w in enumerate(POOL_WINDOWS):
        cs = slice(g * gw, (g + 1) * gw)
        cur = ext_scr[HIST_ROWS:HIST_ROWS + tm, cs]
        win = cur
        for back in range(1, w):
            win = win + ext_scr[HIST_ROWS - back:HIST_ROWS - back + tm, cs]
        cnt = jnp.minimum(w, pos + 1).astype(F32)
        p = win / cnt - cur
        y = _dot(p.astype(BF16), wp_ref[g])
        pb_ref[:, cs] = (y * ps_ref[:, cs]).astype(BF16)


def _pool(u, hist, wp_bf, ps, *, n_seq, seq_len, tm, start):
    m, pw = u.shape
    tiles = seq_len // tm
    n_groups, gw, _ = wp_bf.shape
    block_bytes = (_nbytes((tm, pw), F32) + _nbytes((HIST_ROWS, pw), F32) + _nbytes(wp_bf.shape, BF16)
                   + _nbytes((tm, pw), BF16))
    return pl.pallas_call(
        functools.partial(_pool_kernel, tm=tm, start=start),
        grid=(n_seq, tiles),
        in_specs=[
            pl.BlockSpec((tm, pw), lambda b, t: (b * tiles + t, 0)),
            pl.BlockSpec((HIST_ROWS, pw), lambda b, t: (b, 0)),
            pl.BlockSpec((n_groups, gw, gw), lambda b, t: (0, 0, 0)),
            pl.BlockSpec((1, pw), lambda b, t: (0, 0)),
        ],
        out_specs=pl.BlockSpec((tm, pw), lambda b, t: (b * tiles + t, 0)),
        out_shape=jax.ShapeDtypeStruct((m, pw), BF16),
        scratch_shapes=[pltpu.VMEM((HIST_ROWS + tm, pw), F32)],
        compiler_params=pltpu.CompilerParams(
            dimension_semantics=("arbitrary", "arbitrary"),
            vmem_limit_bytes=_vmem_limit(block_bytes, _nbytes((HIST_ROWS + tm, pw), F32))),
        name="pool_mix",
    )(u, hist, wp_bf, ps)


def _merge_kernel(o_ref, pb_ref, sig_ref, x_ref, wua_ref, wup_ref, wout_ref, x1_ref, m_scr, *, n_heads):
    d = x_ref.shape[1]
    o = jnp.concatenate([o_ref[h] for h in range(n_heads)], axis=1)
    pb = pb_ref[...]
    for n in range(d // FF_TILE):
        cs = slice(n * FF_TILE, (n + 1) * FF_TILE)
        gs = slice(d + n * FF_TILE, d + (n + 1) * FF_TILE)
        a = _dot(o, wua_ref[:, cs])
        b = _dot(pb, wup_ref[:, cs])
        m_scr[:, cs] = (sig_ref[:, cs].astype(F32) * a + sig_ref[:, gs].astype(F32) * b).astype(BF16)
    for n in range(d // FF_TILE):
        cs = slice(n * FF_TILE, (n + 1) * FF_TILE)
        x1_ref[:, cs] = x_ref[:, cs] + _dot(m_scr[...], wout_ref[:, cs])


def _merge(o, pb, sig, x, wua_bf, wup_bf, wout_bf, *, tm):
    n_heads = o.shape[0]
    m, d = x.shape
    aw = n_heads * HEAD_W
    block_bytes = (2 * _nbytes((tm, aw), BF16) + _nbytes((tm, 2 * d), BF16) + 2 * _nbytes((tm, d), F32)
                   + 2 * _nbytes((aw, d), BF16) + _nbytes((d, d), BF16))
    return pl.pallas_call(
        functools.partial(_merge_kernel, n_heads=n_heads),
        grid=(m // tm,),
        in_specs=[
            pl.BlockSpec((n_heads, tm, HEAD_W), lambda i: (0, i, 0)),
            pl.BlockSpec((tm, aw), lambda i: (i, 0)),
            pl.BlockSpec((tm, 2 * d), lambda i: (i, 0)),
            pl.BlockSpec((tm, d), lambda i: (i, 0)),
            pl.BlockSpec((aw, d), lambda i: (0, 0)),
            pl.BlockSpec((aw, d), lambda i: (0, 0)),
            pl.BlockSpec((d, d), lambda i: (0, 0)),
        ],
        out_specs=pl.BlockSpec((tm, d), lambda i: (i, 0)),
        out_shape=jax.ShapeDtypeStruct((m, d), F32),
        scratch_shapes=[pltpu.VMEM((tm, d), BF16)],
        compiler_params=pltpu.CompilerParams(
            dimension_semantics=("arbitrary",),
            vmem_limit_bytes=_vmem_limit(block_bytes, _nbytes((tm, d), BF16))),
        name="merge_out_proj",
    )(o, pb, sig, x, wua_bf, wup_bf, wout_bf)


def _ffn_kernel(*refs, tm, seq_len, streaming):
    if streaming:
        (x_ref, xp_ref, g_ref, wz_ref, wv_ref, cw_ref, cb_ref, wo_ref,
         y_ref, zt_ref, h_scr, hp_scr, zext_scr, acc_scr) = refs
    else:
        (x_ref, zs1_ref, zs2_ref, g_ref, wz_ref, wv_ref, cw_ref, cb_ref, wo_ref,
         y_ref, zt_ref, h_scr, zext_scr, acc_scr) = refs
    i = pl.program_id(0)
    f = pl.program_id(1)
    tf = wz_ref.shape[1]

    @pl.when(f == 0)
    def _():
        h_scr[...] = _rms_rows(x_ref[...], g_ref[...]).astype(BF16)
        if streaming:
            hp_scr[...] = _rms_rows(xp_ref[...], g_ref[...]).astype(BF16)

    h = h_scr[...]
    z = _dot(h, wz_ref[...])
    val = _dot(h, wv_ref[...])
    zext_scr[HIST_ROWS:HIST_ROWS + tm] = z
    if streaming:
        zp = _dot(hp_scr[...], wz_ref[...])
        zext_scr[0:HIST_ROWS] = jnp.where(i > 0, zp, 0.0)
        z1 = zext_scr[HIST_ROWS - 1:HIST_ROWS - 1 + tm]
        z2 = zext_scr[HIST_ROWS - 2:HIST_ROWS - 2 + tm]
        zt_ref[...] = z[tm - 8:tm]
    else:
        zext_scr[0:HIST_ROWS] = jnp.zeros((HIST_ROWS, tf), F32)
        r = lax.broadcasted_iota(jnp.int32, (tm, 1), 0) % seq_len
        z1 = jnp.where(r >= 1, zext_scr[HIST_ROWS - 1:HIST_ROWS - 1 + tm], 0.0) + zs1_ref[...]
        z2 = jnp.where(r >= 2, zext_scr[HIST_ROWS - 2:HIST_ROWS - 2 + tm], 0.0) + zs2_ref[...]
        zt_ref[...] = z
    zc = cb_ref[...] + z2 * cw_ref[0:1] + z1 * cw_ref[1:2] + z * cw_ref[2:3]
    act = (jax.nn.silu(zc) * val).astype(BF16)
    contrib = _dot(act, wo_ref[...])

    @pl.when(f == 0)
    def _():
        acc_scr[...] = contrib

    @pl.when(f > 0)
    def _():
        acc_scr[...] += contrib

    @pl.when(f == pl.num_programs(1) - 1)
    def _():
        y_ref[...] = x_ref[...] + acc_scr[...]


def _ffn(x1, g, wz_bf, wv_bf, cw, cb, wo_bf, *, tm, seq_len, zs1=None, zs2=None):
    m, d = x1.shape
    fp = wz_bf.shape[1]
    tf = FF_TILE
    streaming = zs1 is None
    row_tile = lambda i, f: (i, 0)
    col_tile = lambda i, f: (0, f)
    w_specs = [
        pl.BlockSpec((1, d), lambda i, f: (0, 0)),
        pl.BlockSpec((d, tf), col_tile),
        pl.BlockSpec((d, tf), col_tile),
        pl.BlockSpec((CONV_WIDTH, tf), col_tile),
        pl.BlockSpec((1, tf), col_tile),
        pl.BlockSpec((tf, d), lambda i, f: (f, 0)),
    ]
    scratch = [pltpu.VMEM((tm, d), BF16)]
    if streaming:
        assert seq_len == m and tm % HIST_ROWS == 0
        per_tile = tm // HIST_ROWS
        in_specs = [pl.BlockSpec((tm, d), row_tile),
                    pl.BlockSpec((HIST_ROWS, d), lambda i, f: (jnp.maximum(i * per_tile - 1, 0), 0))] + w_specs
        args = (x1, x1, g, wz_bf, wv_bf, cw, cb, wo_bf)
        zt_shape, zt_spec = (m // tm * 8, fp), pl.BlockSpec((8, tf), lambda i, f: (i, f))
        scratch.append(pltpu.VMEM((HIST_ROWS, d), BF16))
    else:
        assert tm % seq_len == 0 and seq_len >= CONV_WIDTH - 1
        in_specs = [pl.BlockSpec((tm, d), row_tile),
                    pl.BlockSpec((tm, tf), lambda i, f: (i, f)),
                    pl.BlockSpec((tm, tf), lambda i, f: (i, f))] + w_specs
        args = (x1, zs1, zs2, g, wz_bf, wv_bf, cw, cb, wo_bf)
        zt_shape, zt_spec = (m, fp), pl.BlockSpec((tm, tf), lambda i, f: (i, f))
    scratch += [pltpu.VMEM((HIST_ROWS + tm, tf), F32), pltpu.VMEM((tm, d), F32)]
    block_bytes = (2 * _nbytes((tm, d), F32) + 2 * _nbytes((d, tf), BF16) + _nbytes((tf, d), BF16)
                   + 3 * _nbytes((tm, tf), F32))
    scratch_bytes = _nbytes((tm, d), BF16) + _nbytes((HIST_ROWS + tm, tf), F32) + _nbytes((tm, d), F32)
    return pl.pallas_call(
        functools.partial(_ffn_kernel, tm=tm, seq_len=seq_len, streaming=streaming),
        grid=(m // tm, fp // tf),
        in_specs=in_specs,
        out_specs=(pl.BlockSpec((tm, d), row_tile), zt_spec),
        out_shape=(jax.ShapeDtypeStruct((m, d), F32), jax.ShapeDtypeStruct(zt_shape, F32)),
        scratch_shapes=scratch,
        compiler_params=pltpu.CompilerParams(
            dimension_semantics=("arbitrary", "arbitrary"),
            vmem_limit_bytes=_vmem_limit(block_bytes, scratch_bytes)),
        name="conv_glu_ffn",
    )(*args)


def _prep_weights(attn_norm_g, w_in, q_norm_g, k_norm_g, lambda_q1, lambda_k1, lambda_q2, lambda_k2,
                  subln_g, w_pool, pool_scale, w_up_attn, w_up_pool, w_out,
                  ffn_norm_g, w_ffn_in, conv_w, conv_b, w_ffn_out, *, n_heads):
    d_ff = conv_w.shape[1]
    fp = -(-d_ff // FF_TILE) * FF_TILE
    pad = fp - d_ff
    lam_vec = jnp.pad(jnp.stack([lambda_q1, lambda_k1, lambda_q2, lambda_k2]).astype(F32),
                      ((0, 0), (0, LANES - lambda_q1.shape[0])))
    group = np.arange(MXU_TILE) // QK_HALF
    gd = jnp.asarray((group[:, None] == group[None, :]).astype(np.float32) / QK_HALF, BF16)
    return dict(
        attn_g=attn_norm_g.reshape(1, -1),
        w_in=w_in.astype(BF16),
        qg=(jnp.tile(q_norm_g.reshape(-1), n_heads) * (QK_HALF ** -0.5)).reshape(1, -1),
        kg=jnp.tile(k_norm_g.reshape(-1), n_heads).reshape(1, -1),
        gd=gd,
        lam_vec=lam_vec,
        sg=subln_g.reshape(1, -1),
        w_pool=w_pool.astype(BF16),
        pool_scale=pool_scale.reshape(1, -1),
        w_up_attn=w_up_attn.astype(BF16),
        w_up_pool=w_up_pool.astype(BF16),
        w_out=w_out.astype(BF16),
        ffn_g=ffn_norm_g.reshape(1, -1),
        wz=jnp.pad(w_ffn_in[:, :d_ff], ((0, 0), (0, pad))).astype(BF16),
        wv=jnp.pad(w_ffn_in[:, d_ff:], ((0, 0), (0, pad))).astype(BF16),
        conv_w=jnp.pad(conv_w, ((0, 0), (0, pad))),
        conv_b=jnp.pad(conv_b, (0, pad)).reshape(1, -1),
        w_ffn_out=jnp.pad(w_ffn_out, ((0, pad), (0, 0))).astype(BF16),
        d_ff=d_ff,
    )


def _layer(x, start, k_hist, v_hist, pool_hist, conv_hist, lam_init, w, *, n_heads, tm, tq):
    n_seq, t_len, d = x.shape
    m = n_seq * t_len
    aw = n_heads * HEAD_W
    d_ff = w["d_ff"]
    x2 = x.reshape(m, d)

    q, k_f32, kb, v_f32, vb, u, sig = _in_proj(
        x2, w["attn_g"], w["w_in"], w["qg"], w["kg"], w["gd"], tm=tm, n_heads=n_heads)

    if k_hist is None:
        o = _attn_prompt(q, kb, vb, w["lam_vec"], w["sg"], tq=tq, tk=tq, lam_init=lam_init)
    else:
        past = k_hist.shape[1]
        o = _attn_sample(q, kb, vb, k_hist.reshape(n_seq, past, aw), v_hist.reshape(n_seq, past, aw),
                         w["lam_vec"], w["sg"], t_new=t_len, lam_init=lam_init)

    hist = jnp.pad(pool_hist, ((0, 0), (HIST_ROWS - POOL_HIST, 0), (0, 0))).reshape(n_seq * HIST_ROWS, -1)
    pool_tm = min(tm, t_len)
    pb = _pool(u, hist, w["w_pool"], w["pool_scale"], n_seq=n_seq, seq_len=t_len, tm=pool_tm, start=start)

    x1 = _merge(o, pb, sig, x2, w["w_up_attn"], w["w_up_pool"], w["w_out"], tm=min(tm, 256))

    if conv_hist is None:
        y, z_tail = _ffn(x1, w["ffn_g"], w["wz"], w["wv"], w["conv_w"], w["conv_b"], w["w_ffn_out"],
                         tm=tm, seq_len=t_len)
        conv_state = z_tail[-(CONV_WIDTH - 1):, :d_ff][None]
    else:
        fpad = w["wz"].shape[1] - d_ff
        zs1 = jnp.pad(conv_hist[:, 1:2], ((0, 0), (0, t_len - 1), (0, fpad))).reshape(m, -1)
        zs2 = jnp.pad(conv_hist, ((0, 0), (0, t_len - (CONV_WIDTH - 1)), (0, fpad))).reshape(m, -1)
        y, z_all = _ffn(x1, w["ffn_g"], w["wz"], w["wv"], w["conv_w"], w["conv_b"], w["w_ffn_out"],
                        tm=tm, seq_len=t_len, zs1=zs1, zs2=zs2)
        conv_state = z_all.reshape(n_seq, t_len, -1)[:, -(CONV_WIDTH - 1):, :d_ff]

    u3 = u.reshape(n_seq, t_len, -1)
    pool_state = jnp.concatenate([pool_hist, u3], axis=1)[:, -POOL_HIST:] if t_len < POOL_HIST else u3[:, -POOL_HIST:]
    return (y.reshape(n_seq, t_len, d),
            k_f32.reshape(n_seq, t_len, n_heads, HEAD_W),
            v_f32.reshape(n_seq, t_len, n_heads, HEAD_W),
            pool_state, conv_state)


def kernel(x_prompt, x_sample, cache_k, cache_v, state_pool, state_conv, attn_norm_g, w_in, q_norm_g, k_norm_g, lambda_q1, lambda_k1, lambda_q2, lambda_k2, subln_g, w_pool, pool_scale, w_up_attn, w_up_pool, w_out, ffn_norm_g, w_ffn_in, conv_w, conv_b, w_ffn_out):
    depth = w_in.shape[0]
    n_heads = cache_k.shape[3]
    past = cache_k.shape[2]
    y_p, y_s = x_prompt, x_sample
    outs = [[] for _ in range(8)]
    for l in range(depth):
        lam_init = 0.8 - 0.6 * math.exp(-0.3 * l)
        w = _prep_weights(attn_norm_g[l], w_in[l], q_norm_g[l], k_norm_g[l], lambda_q1[l], lambda_k1[l],
                          lambda_q2[l], lambda_k2[l], subln_g[l], w_pool[l], pool_scale[l], w_up_attn[l],
                          w_up_pool[l], w_out[l], ffn_norm_g[l], w_ffn_in[l], conv_w[l], conv_b[l],
                          w_ffn_out[l], n_heads=n_heads)
        pool0 = jnp.zeros((x_prompt.shape[0], POOL_HIST, state_pool.shape[-1]), F32)
        y_p, kp, vp, pp, cp = _layer(y_p, 0, None, None, pool0, None, lam_init, w,
                                     n_heads=n_heads, tm=512, tq=512)
        y_s, ks, vs, ps, cs = _layer(y_s, past, cache_k[l], cache_v[l], state_pool[l], state_conv[l],
                                     lam_init, w, n_heads=n_heads,
                                     tm=x_sample.shape[0] * x_sample.shape[1], tq=None)
        for lst, val in zip(outs, (kp, vp, pp, cp, ks, vs, ps, cs)):
            lst.append(val)
    return (y_p, y_s) + tuple(jnp.stack(lst) for lst in outs)
```

```python
import functools
import math

import numpy as np
import jax
import jax.numpy as jnp
from jax import lax
from jax.experimental import pallas as pl
from jax.experimental.pallas import tpu as pltpu

F32 = jnp.float32
BF16 = jnp.bfloat16

EPS = 1e-6
NEG_INF = -1e30
CHUNK = 64
POOL_WINDOWS = (2, 4, 8, 16)
POOL_HIST = max(POOL_WINDOWS) - 1
CONV_WIDTH = 3

LANES = 128
HIST_ROWS = 16
MXU_TILE = 256
VMEM_LIMIT_CAP = 60 * 1024 * 1024
VMEM_TEMP_ALLOWANCE = 12 * 1024 * 1024

HEAD_W = 128
QK_HALF = 64
FF_TILE = 512
MAX_FIXED_SHIFT = 40.0
BOUND_MARGIN = 1.02


def _vmem_limit(block_bytes, scratch_bytes):
    est = 2 * block_bytes + scratch_bytes + VMEM_TEMP_ALLOWANCE
    return int(min(est, VMEM_LIMIT_CAP))


def _nbytes(shape, dtype):
    return int(np.prod(shape)) * jnp.dtype(dtype).itemsize


def _rms_rows(x, g):
    ms = jnp.mean(x * x, axis=-1, keepdims=True)
    return x * lax.rsqrt(ms + EPS) * g


def _dot(a, b):
    return jnp.dot(a, b, preferred_element_type=F32)


def _dot_nt(a, b):
    return lax.dot_general(a, b, (((1,), (1,)), ((), ())), preferred_element_type=F32)


def _inproj_kernel(x_ref, g_ref, w_ref, qg_ref, kg_ref, gd_ref,
                   q_ref, kf_ref, kb_ref, vf_ref, vb_ref, u_ref, sig_ref, h_scr,
                   *, n_heads):
    j = pl.program_id(1)
    tm = x_ref.shape[0]

    @pl.when(j == 0)
    def _():
        h_scr[...] = _rms_rows(x_ref[...], g_ref[...]).astype(BF16)

    y = _dot(h_scr[...], w_ref[...])

    def qk_norm(gain_ref):
        heads = []
        for c in range(n_heads * HEAD_W // MXU_TILE):
            cs = slice(c * MXU_TILE, (c + 1) * MXU_TILE)
            yc = y[:, cs]
            ms = _dot((yc * yc).astype(BF16), gd_ref[...])
            yn = yc * lax.rsqrt(ms + EPS) * gain_ref[:, cs]
            heads += [yn[:, :HEAD_W], yn[:, HEAD_W:]]
        return heads

    @pl.when(j == 0)
    def _():
        lane = lax.broadcasted_iota(jnp.int32, (tm, HEAD_W), 1)
        for h, yh in enumerate(qk_norm(qg_ref)):
            q_ref[h, 0] = jnp.where(lane < QK_HALF, yh, 0.0).astype(BF16)
            q_ref[h, 1] = jnp.where(lane >= QK_HALF, yh, 0.0).astype(BF16)

    @pl.when(j == 1)
    def _():
        for h, yh in enumerate(qk_norm(kg_ref)):
            kf_ref[:, h, :] = yh
            kb_ref[h] = yh.astype(BF16)

    @pl.when(j == 2)
    def _():
        for h in range(n_heads):
            yh = y[:, h * HEAD_W:(h + 1) * HEAD_W]
            vf_ref[:, h, :] = yh
            vb_ref[h] = yh.astype(BF16)

    @pl.when(j == 3)
    def _():
        u_ref[...] = y

    @pl.when(j >= 4)
    def _():
        sig_ref[...] = jax.nn.sigmoid(y).astype(BF16)


def _in_proj(x, g, w_bf, qg, kg, gd, *, tm, n_heads):
    m, d = x.shape
    aw = n_heads * HEAD_W
    n_col = w_bf.shape[1] // aw
    gate_tiles = n_col - 4
    out_shape = (
        jax.ShapeDtypeStruct((n_heads, 2, m, HEAD_W), BF16),
        jax.ShapeDtypeStruct((m, n_heads, HEAD_W), F32),
        jax.ShapeDtypeStruct((n_heads, m, HEAD_W), BF16),
        jax.ShapeDtypeStruct((m, n_heads, HEAD_W), F32),
        jax.ShapeDtypeStruct((n_heads, m, HEAD_W), BF16),
        jax.ShapeDtypeStruct((m, aw), F32),
        jax.ShapeDtypeStruct((m, gate_tiles * aw), BF16),
    )
    in_specs = [
        pl.BlockSpec((tm, d), lambda i, j: (i, 0)),
        pl.BlockSpec((1, d), lambda i, j: (0, 0)),
        pl.BlockSpec((d, aw), lambda i, j: (0, j)),
        pl.BlockSpec((1, aw), lambda i, j: (0, 0)),
        pl.BlockSpec((1, aw), lambda i, j: (0, 0)),
        pl.BlockSpec((MXU_TILE, MXU_TILE), lambda i, j: (0, 0)),
    ]
    out_specs = (
        pl.BlockSpec((n_heads, 2, tm, HEAD_W), lambda i, j: (0, 0, i, 0)),
        pl.BlockSpec((tm, n_heads, HEAD_W), lambda i, j: (i, 0, 0)),
        pl.BlockSpec((n_heads, tm, HEAD_W), lambda i, j: (0, i, 0)),
        pl.BlockSpec((tm, n_heads, HEAD_W), lambda i, j: (i, 0, 0)),
        pl.BlockSpec((n_heads, tm, HEAD_W), lambda i, j: (0, i, 0)),
        pl.BlockSpec((tm, aw), lambda i, j: (i, 0)),
        pl.BlockSpec((tm, aw), lambda i, j: (i, jnp.maximum(j - 4, 0))),
    )
    block_bytes = (_nbytes((tm, d), F32) + _nbytes((d, aw), BF16) + 3 * _nbytes((tm, aw), F32)
                   + 5 * _nbytes((tm, aw), BF16))
    return pl.pallas_call(
        functools.partial(_inproj_kernel, n_heads=n_heads),
        grid=(m // tm, n_col),
        in_specs=in_specs,
        out_specs=out_specs,
        out_shape=out_shape,
        scratch_shapes=[pltpu.VMEM((tm, d), BF16)],
        compiler_params=pltpu.CompilerParams(
            dimension_semantics=("arbitrary", "arbitrary"),
            vmem_limit_bytes=_vmem_limit(block_bytes, _nbytes((tm, d), BF16))),
        name="in_proj",
    )(x, g, w_bf, qg, kg, gd)


def _lambda_value(lam_ref, lam_init):
    a = lam_ref[...]
    s1 = jnp.sum(a[0:1] * a[1:2], axis=1, keepdims=True)
    s2 = jnp.sum(a[2:3] * a[3:4], axis=1, keepdims=True)
    return jnp.exp(s1) - jnp.exp(s2) + lam_init


def _sub_ln(o, sg, lam_init):
    o = o * lax.rsqrt(jnp.mean(o * o, axis=-1, keepdims=True) + EPS)
    return o * sg * (1.0 - lam_init)


def _attn_kernel(qi_ref, kj_ref, flag_ref, fast_ref, bound_ref, lam_ref, sg_ref, q_ref, k_ref, v_ref, o_ref,
                 m_scr, l_scr, acc_scr, *, tq, tk, n_heads, lam_init):
    t = pl.program_id(0)
    i = qi_ref[t]
    j = kj_ref[t]
    flags = flag_ref[t]
    is_first = (flags & 1) != 0
    is_last = (flags & 2) != 0
    is_masked = (flags & 4) != 0
    fast = fast_ref[0] != 0

    @pl.when(is_first)
    def _():
        m_scr[...] = jnp.full(m_scr.shape, NEG_INF, F32)
        l_scr[...] = jnp.zeros(l_scr.shape, F32)
        acc_scr[...] = jnp.zeros(acc_scr.shape, F32)

    def step(masked, bounded):
        if masked:
            qpos = i * tq + lax.broadcasted_iota(jnp.int32, (tq, tk), 0)
            kpos = j * tk + lax.broadcasted_iota(jnp.int32, (tq, tk), 1)
            mask = (kpos // CHUNK) <= (qpos // CHUNK)

        def head(h, carry):
            kh = k_ref[h]
            vh = v_ref[h]
            for c in range(2):
                s = _dot_nt(q_ref[h, c], kh)
                if bounded:
                    s = s - bound_ref[c]
                if masked:
                    s = jnp.where(mask, s, NEG_INF)
                if bounded:
                    p = jnp.exp(s)
                    psum = p[:, :LANES]
                    for n in range(1, tk // LANES):
                        psum = psum + p[:, n * LANES:(n + 1) * LANES]
                    l_scr[h, c] += psum
                    acc_scr[h, c] += _dot(p.astype(BF16), vh)
                else:
                    m_prev = m_scr[h, c]
                    m_next = jnp.maximum(m_prev, jnp.max(s, axis=1, keepdims=True))
                    p = jnp.exp(s - jnp.tile(m_next, (1, tk // LANES)))
                    alpha = jnp.exp(m_prev - m_next)
                    l_scr[h, c] = alpha * l_scr[h, c] + jnp.sum(p, axis=1, keepdims=True)
                    m_scr[h, c] = m_next
                    acc_scr[h, c] = alpha * acc_scr[h, c] + _dot(p.astype(BF16), vh)
            return carry

        lax.fori_loop(0, n_heads, head, 0, unroll=bounded and not masked)

    for masked in (False, True):
        for bounded in (False, True):
            pl.when((is_masked == masked) & (fast == bounded))(functools.partial(step, masked, bounded))

    def finalize(bounded):
        lam = _lambda_value(lam_ref, lam_init)

        def row_sum(h, c):
            l = l_scr[h, c]
            return jnp.sum(l, axis=1, keepdims=True) if bounded else l

        def head(h, carry):
            o = acc_scr[h, 0] / row_sum(h, 0) - lam * (acc_scr[h, 1] / row_sum(h, 1))
            o_ref[h] = _sub_ln(o, sg_ref[...], lam_init).astype(BF16)
            return carry

        lax.fori_loop(0, n_heads, head, 0)

    for bounded in (False, True):
        pl.when(is_last & (fast == bounded))(functools.partial(finalize, bounded))


def _attn_schedule(seq, tq, tk):
    qi, kj, flags = [], [], []
    for i in range(seq // tq):
        q_lo, q_hi = i * tq, (i + 1) * tq - 1
        js = [j for j in range(seq // tk) if (j * tk) // CHUNK <= q_hi // CHUNK]
        for n, j in enumerate(js):
            masked = ((j + 1) * tk - 1) // CHUNK > q_lo // CHUNK
            qi.append(i)
            kj.append(j)
            flags.append((1 if n == 0 else 0) | (2 if n == len(js) - 1 else 0) | (4 if masked else 0))
    return (np.asarray(qi, np.int32), np.asarray(kj, np.int32), np.asarray(flags, np.int32))


def _attn_prompt(q, kb, vb, lam_vec, sg, score_bound, use_bound, *, tq, tk, lam_init):
    n_heads, _, seq, _ = q.shape
    qi, kj, flags = _attn_schedule(seq, tq, tk)
    stat = (n_heads, 2, tq, LANES)
    grid_spec = pltpu.PrefetchScalarGridSpec(
        num_scalar_prefetch=5,
        grid=(len(qi),),
        in_specs=[
            pl.BlockSpec((4, LANES), lambda t, qi, kj, *_: (0, 0)),
            pl.BlockSpec((1, HEAD_W), lambda t, qi, kj, *_: (0, 0)),
            pl.BlockSpec((n_heads, 2, tq, HEAD_W), lambda t, qi, kj, *_: (0, 0, qi[t], 0)),
            pl.BlockSpec((n_heads, tk, HEAD_W), lambda t, qi, kj, *_: (0, kj[t], 0)),
            pl.BlockSpec((n_heads, tk, HEAD_W), lambda t, qi, kj, *_: (0, kj[t], 0)),
        ],
        out_specs=pl.BlockSpec((n_heads, tq, HEAD_W), lambda t, qi, kj, *_: (0, qi[t], 0)),
        scratch_shapes=[pltpu.VMEM(stat, F32), pltpu.VMEM(stat, F32), pltpu.VMEM(stat, F32)],
    )
    block_bytes = (_nbytes((n_heads, 2, tq, HEAD_W), BF16) + 2 * _nbytes((n_heads, tk, HEAD_W), BF16)
                   + _nbytes((n_heads, tq, HEAD_W), BF16))
    return pl.pallas_call(
        functools.partial(_attn_kernel, tq=tq, tk=tk, n_heads=n_heads, lam_init=lam_init),
        grid_spec=grid_spec,
        out_shape=jax.ShapeDtypeStruct((n_heads, seq, HEAD_W), BF16),
        compiler_params=pltpu.CompilerParams(
            dimension_semantics=("arbitrary",),
            vmem_limit_bytes=_vmem_limit(block_bytes, 3 * _nbytes(stat, F32))),
        name="attn_prompt",
    )(jnp.asarray(qi), jnp.asarray(kj), jnp.asarray(flags), use_bound, score_bound, lam_vec, sg, q, kb, vb)


def _attn_dec_kernel(lam_ref, sg_ref, q_ref, kn_ref, vn_ref, ck_ref, cv_ref, o_ref,
                     *, past, t_new, n_heads, lam_init):
    lam = _lambda_value(lam_ref, lam_init)
    pad_rows = LANES - t_new
    n_keys = past + LANES
    qpos = past + lax.broadcasted_iota(jnp.int32, (t_new, n_keys), 0)
    kpos = lax.broadcasted_iota(jnp.int32, (t_new, n_keys), 1)
    mask = (kpos < past + t_new) & ((kpos // CHUNK) <= (qpos // CHUNK))
    zpad = jnp.zeros((pad_rows, HEAD_W), BF16)
    for h in range(n_heads):
        k_all = jnp.concatenate([ck_ref[:, h, :].astype(BF16), kn_ref[h], zpad], axis=0)
        v_all = jnp.concatenate([cv_ref[:, h, :].astype(BF16), vn_ref[h], zpad], axis=0)
        comps = []
        for c in range(2):
            s = jnp.where(mask, _dot_nt(q_ref[h, c], k_all), NEG_INF)
            m = jnp.max(s, axis=1, keepdims=True)
            p = jnp.exp(s - m)
            l = jnp.sum(p, axis=1, keepdims=True)
            comps.append(_dot(p.astype(BF16), v_all) / l)
        o = comps[0] - lam * comps[1]
        o_ref[h] = _sub_ln(o, sg_ref[...], lam_init).astype(BF16)


def _attn_sample(q, kb, vb, cache_k, cache_v, lam_vec, sg, *, layer, t_new, lam_init):
    n_heads = q.shape[0]
    _, n_seq, past, _, _ = cache_k.shape
    aw = n_heads * HEAD_W
    cache_spec = pl.BlockSpec((None, None, past, n_heads, HEAD_W), lambda b: (layer, b, 0, 0, 0))
    block_bytes = 2 * _nbytes((past, aw), F32) + 5 * _nbytes((n_heads, t_new, HEAD_W), BF16)
    return pl.pallas_call(
        functools.partial(_attn_dec_kernel, past=past, t_new=t_new, n_heads=n_heads, lam_init=lam_init),
        grid=(n_seq,),
        in_specs=[
            pl.BlockSpec((4, LANES), lambda b: (0, 0)),
            pl.BlockSpec((1, HEAD_W), lambda b: (0, 0)),
            pl.BlockSpec((n_heads, 2, t_new, HEAD_W), lambda b: (0, 0, b, 0)),
            pl.BlockSpec((n_heads, t_new, HEAD_W), lambda b: (0, b, 0)),
            pl.BlockSpec((n_heads, t_new, HEAD_W), lambda b: (0, b, 0)),
            cache_spec,
            cache_spec,
        ],
        out_specs=pl.BlockSpec((n_heads, t_new, HEAD_W), lambda b: (0, b, 0)),
        out_shape=jax.ShapeDtypeStruct((n_heads, n_seq * t_new, HEAD_W), BF16),
        compiler_params=pltpu.CompilerParams(
            dimension_semantics=("arbitrary",),
            vmem_limit_bytes=_vmem_limit(block_bytes, 0)),
        name="attn_sample",
    )(lam_vec, sg, q, kb, vb, cache_k, cache_v)


def _pool_kernel(u_ref, hist_ref, wp_ref, ps_ref, pb_ref, ext_scr, *, tm, start):
    ti = pl.program_id(1)

    @pl.when(ti == 0)
    def _():
        ext_scr[0:HIST_ROWS] = hist_ref[...]

    @pl.when(ti > 0)
    def _():
        ext_scr[0:HIST_ROWS] = ext_scr[tm:tm + HIST_ROWS]

    ext_scr[HIST_ROWS:HIST_ROWS + tm] = u_ref[...]
    pos = start + ti * tm + lax.broadcasted_iota(jnp.int32, (tm, 1), 0)
    gw = wp_ref.shape[1]
    for g, w in enumerate(POOL_WINDOWS):
        cs = slice(g * gw, (g + 1) * gw)
        cur = ext_scr[HIST_ROWS:HIST_ROWS + tm, cs]
        win = cur
        for back in range(1, w):
            win = win + ext_scr[HIST_ROWS - back:HIST_ROWS - back + tm, cs]
        cnt = jnp.minimum(w, pos + 1).astype(F32)
        p = win / cnt - cur
        y = _dot(p.astype(BF16), wp_ref[g])
        pb_ref[:, cs] = (y * ps_ref[:, cs]).astype(BF16)


def _pool(u, hist, wp_bf, ps, *, n_seq, seq_len, tm, start):
    m, pw = u.shape
    tiles = seq_len // tm
    n_groups, gw, _ = wp_bf.shape
    block_bytes = (_nbytes((tm, pw), F32) + _nbytes((HIST_ROWS, pw), F32) + _nbytes(wp_bf.shape, BF16)
                   + _nbytes((tm, pw), BF16))
    return pl.pallas_call(
        functools.partial(_pool_kernel, tm=tm, start=start),
        grid=(n_seq, tiles),
        in_specs=[
            pl.BlockSpec((tm, pw), lambda b, t: (b * tiles + t, 0)),
            pl.BlockSpec((HIST_ROWS, pw), lambda b, t: (b, 0)),
            pl.BlockSpec((n_groups, gw, gw), lambda b, t: (0, 0, 0)),
            pl.BlockSpec((1, pw), lambda b, t: (0, 0)),
        ],
        out_specs=pl.BlockSpec((tm, pw), lambda b, t: (b * tiles + t, 0)),
        out_shape=jax.ShapeDtypeStruct((m, pw), BF16),
        scratch_shapes=[pltpu.VMEM((HIST_ROWS + tm, pw), F32)],
        compiler_params=pltpu.CompilerParams(
            dimension_semantics=("arbitrary", "arbitrary"),
            vmem_limit_bytes=_vmem_limit(block_bytes, _nbytes((HIST_ROWS + tm, pw), F32))),
        name="pool_mix",
    )(u, hist, wp_bf, ps)


def _merge_kernel(o_ref, pb_ref, sig_ref, x_ref, wua_ref, wup_ref, wout_ref, x1_ref, m_scr, *, n_heads):
    d = x_ref.shape[1]
    o = jnp.concatenate([o_ref[h] for h in range(n_heads)], axis=1)
    pb = pb_ref[...]
    for n in range(d // FF_TILE):
        cs = slice(n * FF_TILE, (n + 1) * FF_TILE)
        gs = slice(d + n * FF_TILE, d + (n + 1) * FF_TILE)
        a = _dot(o, wua_ref[:, cs])
        b = _dot(pb, wup_ref[:, cs])
        m_scr[:, cs] = (sig_ref[:, cs].astype(F32) * a + sig_ref[:, gs].astype(F32) * b).astype(BF16)
    for n in range(d // FF_TILE):
        cs = slice(n * FF_TILE, (n + 1) * FF_TILE)
        x1_ref[:, cs] = x_ref[:, cs] + _dot(m_scr[...], wout_ref[:, cs])


def _merge(o, pb, sig, x, wua_bf, wup_bf, wout_bf, *, tm):
    n_heads = o.shape[0]
    m, d = x.shape
    aw = n_heads * HEAD_W
    block_bytes = (2 * _nbytes((tm, aw), BF16) + _nbytes((tm, 2 * d), BF16) + 2 * _nbytes((tm, d), F32)
                   + 2 * _nbytes((aw, d), BF16) + _nbytes((d, d), BF16))
    return pl.pallas_call(
        functools.partial(_merge_kernel, n_heads=n_heads),
        grid=(m // tm,),
        in_specs=[
            pl.BlockSpec((n_heads, tm, HEAD_W), lambda i: (0, i, 0)),
            pl.BlockSpec((tm, aw), lambda i: (i, 0)),
            pl.BlockSpec((tm, 2 * d), lambda i: (i, 0)),
            pl.BlockSpec((tm, d), lambda i: (i, 0)),
            pl.BlockSpec((aw, d), lambda i: (0, 0)),
            pl.BlockSpec((aw, d), lambda i: (0, 0)),
            pl.BlockSpec((d, d), lambda i: (0, 0)),
        ],
        out_specs=pl.BlockSpec((tm, d), lambda i: (i, 0)),
        out_shape=jax.ShapeDtypeStruct((m, d), F32),
        scratch_shapes=[pltpu.VMEM((tm, d), BF16)],
        compiler_params=pltpu.CompilerParams(
            dimension_semantics=("arbitrary",),
            vmem_limit_bytes=_vmem_limit(block_bytes, _nbytes((tm, d), BF16))),
        name="merge_out_proj",
    )(o, pb, sig, x, wua_bf, wup_bf, wout_bf)


def _ffn_kernel(*refs, tm, seq_len, streaming):
    if streaming:
        (x_ref, xp_ref, g_ref, wz_ref, wv_ref, cw_ref, cb_ref, wo_ref,
         y_ref, zt_ref, h_scr, hp_scr, zext_scr, acc_scr) = refs
    else:
        (x_ref, zs1_ref, zs2_ref, g_ref, wz_ref, wv_ref, cw_ref, cb_ref, wo_ref,
         y_ref, zt_ref, h_scr, zext_scr, acc_scr) = refs
    i = pl.program_id(0)
    f = pl.program_id(1)
    tf = wz_ref.shape[1]

    @pl.when(f == 0)
    def _():
        h_scr[...] = _rms_rows(x_ref[...], g_ref[...]).astype(BF16)
        if streaming:
            hp_scr[...] = _rms_rows(xp_ref[...], g_ref[...]).astype(BF16)

    h = h_scr[...]
    z = _dot(h, wz_ref[...])
    val = _dot(h, wv_ref[...])
    zext_scr[HIST_ROWS:HIST_ROWS + tm] = z
    if streaming:
        zp = _dot(hp_scr[...], wz_ref[...])
        zext_scr[0:HIST_ROWS] = jnp.where(i > 0, zp, 0.0)
        z1 = zext_scr[HIST_ROWS - 1:HIST_ROWS - 1 + tm]
        z2 = zext_scr[HIST_ROWS - 2:HIST_ROWS - 2 + tm]
        zt_ref[...] = z[tm - 8:tm]
    else:
        zext_scr[0:HIST_ROWS] = jnp.zeros((HIST_ROWS, tf), F32)
        r = lax.broadcasted_iota(jnp.int32, (tm, 1), 0) % seq_len
        z1 = jnp.where(r >= 1, zext_scr[HIST_ROWS - 1:HIST_ROWS - 1 + tm], 0.0) + zs1_ref[...]
        z2 = jnp.where(r >= 2, zext_scr[HIST_ROWS - 2:HIST_ROWS - 2 + tm], 0.0) + zs2_ref[...]
        zt_ref[...] = z
    zc = cb_ref[...] + z2 * cw_ref[0:1] + z1 * cw_ref[1:2] + z * cw_ref[2:3]
    act = (jax.nn.silu(zc) * val).astype(BF16)
    contrib = _dot(act, wo_ref[...])

    @pl.when(f == 0)
    def _():
        acc_scr[...] = contrib

    @pl.when(f > 0)
    def _():
        acc_scr[...] += contrib

    @pl.when(f == pl.num_programs(1) - 1)
    def _():
        y_ref[...] = x_ref[...] + acc_scr[...]


def _ffn(x1, g, wzv_bf, cw, cb, wo_bf, *, tm, seq_len, zs1=None, zs2=None):
    m, d = x1.shape
    fp = wzv_bf.shape[1] // 2
    tf = FF_TILE
    n_f = fp // tf
    wz_bf = wv_bf = wzv_bf
    streaming = zs1 is None
    row_tile = lambda i, f: (i, 0)
    col_tile = lambda i, f: (0, f)
    w_specs = [
        pl.BlockSpec((1, d), lambda i, f: (0, 0)),
        pl.BlockSpec((d, tf), col_tile),
        pl.BlockSpec((d, tf), lambda i, f: (0, n_f + f)),
        pl.BlockSpec((CONV_WIDTH, tf), col_tile),
        pl.BlockSpec((1, tf), col_tile),
        pl.BlockSpec((tf, d), lambda i, f: (f, 0)),
    ]
    scratch = [pltpu.VMEM((tm, d), BF16)]
    if streaming:
        assert seq_len == m and tm % HIST_ROWS == 0
        per_tile = tm // HIST_ROWS
        in_specs = [pl.BlockSpec((tm, d), row_tile),
                    pl.BlockSpec((HIST_ROWS, d), lambda i, f: (jnp.maximum(i * per_tile - 1, 0), 0))] + w_specs
        args = (x1, x1, g, wz_bf, wv_bf, cw, cb, wo_bf)
        zt_shape, zt_spec = (m // tm * 8, fp), pl.BlockSpec((8, tf), lambda i, f: (i, f))
        scratch.append(pltpu.VMEM((HIST_ROWS, d), BF16))
    else:
        assert tm % seq_len == 0 and seq_len >= CONV_WIDTH - 1
        in_specs = [pl.BlockSpec((tm, d), row_tile),
                    pl.BlockSpec((tm, tf), lambda i, f: (i, f)),
                    pl.BlockSpec((tm, tf), lambda i, f: (i, f))] + w_specs
        args = (x1, zs1, zs2, g, wz_bf, wv_bf, cw, cb, wo_bf)
        zt_shape, zt_spec = (m, fp), pl.BlockSpec((tm, tf), lambda i, f: (i, f))
    scratch += [pltpu.VMEM((HIST_ROWS + tm, tf), F32), pltpu.VMEM((tm, d), F32)]
    block_bytes = (2 * _nbytes((tm, d), F32) + 2 * _nbytes((d, tf), BF16) + _nbytes((tf, d), BF16)
                   + 3 * _nbytes((tm, tf), F32))
    scratch_bytes = _nbytes((tm, d), BF16) + _nbytes((HIST_ROWS + tm, tf), F32) + _nbytes((tm, d), F32)
    return pl.pallas_call(
        functools.partial(_ffn_kernel, tm=tm, seq_len=seq_len, streaming=streaming),
        grid=(m // tm, fp // tf),
        in_specs=in_specs,
        out_specs=(pl.BlockSpec((tm, d), row_tile), zt_spec),
        out_shape=(jax.ShapeDtypeStruct((m, d), F32), jax.ShapeDtypeStruct(zt_shape, F32)),
        scratch_shapes=scratch,
        compiler_params=pltpu.CompilerParams(
            dimension_semantics=("arbitrary", "arbitrary"),
            vmem_limit_bytes=_vmem_limit(block_bytes, scratch_bytes)),
        name="conv_glu_ffn",
    )(*args)


def _prep_weights(attn_norm_g, w_in, q_norm_g, k_norm_g, lambda_q1, lambda_k1, lambda_q2, lambda_k2,
                  subln_g, w_pool, pool_scale, w_up_attn, w_up_pool, w_out,
                  ffn_norm_g, w_ffn_in, conv_w, conv_b, w_ffn_out, *, n_heads):
    d_ff = conv_w.shape[1]
    fp = -(-d_ff // FF_TILE) * FF_TILE
    pad = fp - d_ff
    lam_vec = jnp.pad(jnp.stack([lambda_q1, lambda_k1, lambda_q2, lambda_k2]).astype(F32),
                      ((0, 0), (0, LANES - lambda_q1.shape[0])))
    group = np.arange(MXU_TILE) // QK_HALF
    gd = jnp.asarray((group[:, None] == group[None, :]).astype(np.float32) / QK_HALF, BF16)
    score_bound = (QK_HALF ** 0.5 * BOUND_MARGIN * jnp.max(jnp.abs(q_norm_g), axis=1)
                   * jnp.max(jnp.abs(k_norm_g), axis=1)).astype(F32)
    use_bound = (jnp.max(score_bound) <= MAX_FIXED_SHIFT).astype(jnp.int32).reshape(1)
    d = w_ffn_in.shape[0]
    wzv = jnp.pad(w_ffn_in.reshape(d, 2, d_ff).astype(BF16), ((0, 0), (0, 0), (0, pad))).reshape(d, 2 * fp)
    return dict(
        score_bound=score_bound,
        use_bound=use_bound,
        attn_g=attn_norm_g.reshape(1, -1),
        w_in=w_in.astype(BF16),
        qg=(jnp.tile(q_norm_g.reshape(-1), n_heads) * (QK_HALF ** -0.5)).reshape(1, -1),
        kg=jnp.tile(k_norm_g.reshape(-1), n_heads).reshape(1, -1),
        gd=gd,
        lam_vec=lam_vec,
        sg=subln_g.reshape(1, -1),
        w_pool=w_pool.astype(BF16),
        pool_scale=pool_scale.reshape(1, -1),
        w_up_attn=w_up_attn.astype(BF16),
        w_up_pool=w_up_pool.astype(BF16),
        w_out=w_out.astype(BF16),
        ffn_g=ffn_norm_g.reshape(1, -1),
        wzv=wzv,
        conv_w=jnp.pad(conv_w, ((0, 0), (0, pad))),
        conv_b=jnp.pad(conv_b, (0, pad)).reshape(1, -1),
        w_ffn_out=jnp.pad(w_ffn_out.astype(BF16), ((0, pad), (0, 0))),
        d_ff=d_ff,
    )


def _layer(x, start, caches, layer, pool_hist, conv_hist, lam_init, w, *, n_heads, tm, tq):
    n_seq, t_len, d = x.shape
    m = n_seq * t_len
    d_ff = w["d_ff"]
    x2 = x.reshape(m, d)

    q, k_f32, kb, v_f32, vb, u, sig = _in_proj(
        x2, w["attn_g"], w["w_in"], w["qg"], w["kg"], w["gd"], tm=tm, n_heads=n_heads)

    if caches is None:
        o = _attn_prompt(q, kb, vb, w["lam_vec"], w["sg"], w["score_bound"], w["use_bound"],
                         tq=tq, tk=tq, lam_init=lam_init)
    else:
        o = _attn_sample(q, kb, vb, caches[0], caches[1], w["lam_vec"], w["sg"],
                         layer=layer, t_new=t_len, lam_init=lam_init)

    hist = jnp.pad(pool_hist, ((0, 0), (HIST_ROWS - POOL_HIST, 0), (0, 0))).reshape(n_seq * HIST_ROWS, -1)
    pool_tm = min(tm, t_len)
    pb = _pool(u, hist, w["w_pool"], w["pool_scale"], n_seq=n_seq, seq_len=t_len, tm=pool_tm, start=start)

    x1 = _merge(o, pb, sig, x2, w["w_up_attn"], w["w_up_pool"], w["w_out"], tm=min(tm, 256))

    if conv_hist is None:
        y, z_tail = _ffn(x1, w["ffn_g"], w["wzv"], w["conv_w"], w["conv_b"], w["w_ffn_out"],
                         tm=tm, seq_len=t_len)
        conv_state = z_tail[-(CONV_WIDTH - 1):, :d_ff][None]
    else:
        fpad = w["conv_b"].shape[1] - d_ff
        zs1 = jnp.pad(conv_hist[:, 1:2], ((0, 0), (0, t_len - 1), (0, fpad))).reshape(m, -1)
        zs2 = jnp.pad(conv_hist, ((0, 0), (0, t_len - (CONV_WIDTH - 1)), (0, fpad))).reshape(m, -1)
        y, z_all = _ffn(x1, w["ffn_g"], w["wzv"], w["conv_w"], w["conv_b"], w["w_ffn_out"],
                        tm=tm, seq_len=t_len, zs1=zs1, zs2=zs2)
        conv_state = z_all.reshape(n_seq, t_len, -1)[:, -(CONV_WIDTH - 1):, :d_ff]

    u3 = u.reshape(n_seq, t_len, -1)
    pool_state = jnp.concatenate([pool_hist, u3], axis=1)[:, -POOL_HIST:] if t_len < POOL_HIST else u3[:, -POOL_HIST:]
    return (y.reshape(n_seq, t_len, d),
            k_f32.reshape(n_seq, t_len, n_heads, HEAD_W),
            v_f32.reshape(n_seq, t_len, n_heads, HEAD_W),
            pool_state, conv_state)


def kernel(x_prompt, x_sample, cache_k, cache_v, state_pool, state_conv, attn_norm_g, w_in, q_norm_g, k_norm_g, lambda_q1, lambda_k1, lambda_q2, lambda_k2, subln_g, w_pool, pool_scale, w_up_attn, w_up_pool, w_out, ffn_norm_g, w_ffn_in, conv_w, conv_b, w_ffn_out):
    depth = w_in.shape[0]
    n_heads = cache_k.shape[3]
    past = cache_k.shape[2]
    y_p, y_s = x_prompt, x_sample
    outs = [[] for _ in range(8)]
    for l in range(depth):
        lam_init = 0.8 - 0.6 * math.exp(-0.3 * l)
        w = _prep_weights(attn_norm_g[l], w_in[l], q_norm_g[l], k_norm_g[l], lambda_q1[l], lambda_k1[l],
                          lambda_q2[l], lambda_k2[l], subln_g[l], w_pool[l], pool_scale[l], w_up_attn[l],
                          w_up_pool[l], w_out[l], ffn_norm_g[l], w_ffn_in[l], conv_w[l], conv_b[l],
                          w_ffn_out[l], n_heads=n_heads)
        pool0 = jnp.zeros((x_prompt.shape[0], POOL_HIST, state_pool.shape[-1]), F32)
        y_p, kp, vp, pp, cp = _layer(y_p, 0, None, l, pool0, None, lam_init, w,
                                     n_heads=n_heads, tm=512, tq=512)
        y_s, ks, vs, ps, cs = _layer(y_s, past, (cache_k, cache_v), l, state_pool[l], state_conv[l],
                                     lam_init, w, n_heads=n_heads,
                                     tm=x_sample.shape[0] * x_sample.shape[1], tq=None)
        for lst, val in zip(outs, (kp, vp, pp, cp, ks, vs, ps, cs)):
            lst.append(val)
    return (y_p, y_s) + tuple(jnp.stack(lst) for lst in outs)
```

```python
import functools
import math

import numpy as np
import jax
import jax.numpy as jnp
from jax import lax
from jax.experimental import pallas as pl
from jax.experimental.pallas import tpu as pltpu

F32 = jnp.float32
BF16 = jnp.bfloat16

EPS = 1e-6
NEG_INF = -1e30
CHUNK = 64
POOL_WINDOWS = (2, 4, 8, 16)
POOL_HIST = max(POOL_WINDOWS) - 1
CONV_WIDTH = 3

LANES = 128
HIST_ROWS = 16
MXU_TILE = 256
VMEM_LIMIT_CAP = 60 * 1024 * 1024
VMEM_TEMP_ALLOWANCE = 12 * 1024 * 1024

HEAD_W = 128
QK_HALF = 64
FF_TILE = 512
SAMPLE_POS_CHUNK = 512
INPROJ_ROWS = 256
MAX_FIXED_SHIFT = 40.0
BOUND_MARGIN = 1.02


def _vmem_limit(block_bytes, scratch_bytes):
    est = 2 * block_bytes + scratch_bytes + VMEM_TEMP_ALLOWANCE
    return int(min(est, VMEM_LIMIT_CAP))


def _nbytes(shape, dtype):
    return int(np.prod(shape)) * jnp.dtype(dtype).itemsize


def _rms_rows(x, g):
    ms = jnp.mean(x * x, axis=-1, keepdims=True)
    return x * lax.rsqrt(ms + EPS) * g


def _dot(a, b):
    return jnp.dot(a, b, preferred_element_type=F32)


def _dot_nt(a, b):
    return lax.dot_general(a, b, (((1,), (1,)), ((), ())), preferred_element_type=F32)


def _inproj_kernel(x_ref, g_ref, w_ref, qg_ref, kg_ref, gd_ref,
                   q_ref, kf_ref, kb_ref, vf_ref, vb_ref, u_ref, sig_ref, h_scr,
                   *, n_heads):
    tm = x_ref.shape[0]
    aw = n_heads * HEAD_W
    n_col = w_ref.shape[1] // aw
    h_scr[...] = _rms_rows(x_ref[...], g_ref[...]).astype(BF16)

    def col_tile(j):
        return _dot(h_scr[...], w_ref[:, j * aw:(j + 1) * aw])

    def qk_norm(y, gain_ref):
        heads = []
        for c in range(aw // MXU_TILE):
            cs = slice(c * MXU_TILE, (c + 1) * MXU_TILE)
            yc = y[:, cs]
            ms = _dot((yc * yc).astype(BF16), gd_ref[...])
            yn = yc * lax.rsqrt(ms + EPS) * gain_ref[:, cs]
            heads += [yn[:, :HEAD_W], yn[:, HEAD_W:]]
        return heads

    for j in range(4, n_col):
        sig_ref[:, (j - 4) * aw:(j - 3) * aw] = jax.nn.sigmoid(col_tile(j)).astype(BF16)

    lane = lax.broadcasted_iota(jnp.int32, (tm, HEAD_W), 1)
    for h, yh in enumerate(qk_norm(col_tile(0), qg_ref)):
        q_ref[h, 0] = jnp.where(lane < QK_HALF, yh, 0.0).astype(BF16)
        q_ref[h, 1] = jnp.where(lane >= QK_HALF, yh, 0.0).astype(BF16)

    for h, yh in enumerate(qk_norm(col_tile(1), kg_ref)):
        kf_ref[:, h * HEAD_W:(h + 1) * HEAD_W] = yh
        kb_ref[h] = yh.astype(BF16)

    y = col_tile(2)
    vf_ref[...] = y
    for h in range(n_heads):
        vb_ref[h] = y[:, h * HEAD_W:(h + 1) * HEAD_W].astype(BF16)

    u_ref[...] = col_tile(3)


def _in_proj(x, g, w_bf, qg, kg, gd, *, tm, n_heads):
    m, d = x.shape
    aw = n_heads * HEAD_W
    n_all = w_bf.shape[1]
    gate_w = n_all - 4 * aw
    out_shape = (
        jax.ShapeDtypeStruct((n_heads, 2, m, HEAD_W), BF16),
        jax.ShapeDtypeStruct((m, aw), F32),
        jax.ShapeDtypeStruct((n_heads, m, HEAD_W), BF16),
        jax.ShapeDtypeStruct((m, aw), F32),
        jax.ShapeDtypeStruct((n_heads, m, HEAD_W), BF16),
        jax.ShapeDtypeStruct((m, aw), F32),
        jax.ShapeDtypeStruct((m, gate_w), BF16),
    )
    resident = pl.Buffered(1)
    in_specs = [
        pl.BlockSpec((tm, d), lambda i: (i, 0)),
        pl.BlockSpec((1, d), lambda i: (0, 0)),
        pl.BlockSpec((d, n_all), lambda i: (0, 0), pipeline_mode=resident),
        pl.BlockSpec((1, aw), lambda i: (0, 0)),
        pl.BlockSpec((1, aw), lambda i: (0, 0)),
        pl.BlockSpec((MXU_TILE, MXU_TILE), lambda i: (0, 0)),
    ]
    out_specs = (
        pl.BlockSpec((n_heads, 2, tm, HEAD_W), lambda i: (0, 0, i, 0)),
        pl.BlockSpec((tm, aw), lambda i: (i, 0)),
        pl.BlockSpec((n_heads, tm, HEAD_W), lambda i: (0, i, 0)),
        pl.BlockSpec((tm, aw), lambda i: (i, 0)),
        pl.BlockSpec((n_heads, tm, HEAD_W), lambda i: (0, i, 0)),
        pl.BlockSpec((tm, aw), lambda i: (i, 0)),
        pl.BlockSpec((tm, gate_w), lambda i: (i, 0)),
    )
    block_bytes = (_nbytes((tm, d), F32) + 3 * _nbytes((tm, aw), F32) + 4 * _nbytes((tm, aw), BF16)
                   + _nbytes((tm, gate_w), BF16))
    return pl.pallas_call(
        functools.partial(_inproj_kernel, n_heads=n_heads),
        grid=(m // tm,),
        in_specs=in_specs,
        out_specs=out_specs,
        out_shape=out_shape,
        scratch_shapes=[pltpu.VMEM((tm, d), BF16)],
        compiler_params=pltpu.CompilerParams(
            dimension_semantics=("arbitrary",),
            vmem_limit_bytes=_vmem_limit(block_bytes, _nbytes((d, n_all), BF16) + _nbytes((tm, d), BF16))),
        name="in_proj",
    )(x, g, w_bf, qg, kg, gd)


def _lambda_value(lam_ref, lam_init):
    a = lam_ref[...]
    s1 = jnp.sum(a[0:1] * a[1:2], axis=1, keepdims=True)
    s2 = jnp.sum(a[2:3] * a[3:4], axis=1, keepdims=True)
    return jnp.exp(s1) - jnp.exp(s2) + lam_init


def _sub_ln(o, sg, lam_init):
    o = o * lax.rsqrt(jnp.mean(o * o, axis=-1, keepdims=True) + EPS)
    return o * sg * (1.0 - lam_init)


def _attn_kernel(qi_ref, kj_ref, flag_ref, fast_ref, bound_ref, lam_ref, sg_ref, q_ref, k_ref, v_ref, o_ref,
                 m_scr, l_scr, acc_scr, *, tq, tk, n_heads, lam_init):
    t = pl.program_id(0)
    i = qi_ref[t]
    j = kj_ref[t]
    flags = flag_ref[t]
    is_first = (flags & 1) != 0
    is_last = (flags & 2) != 0
    is_masked = (flags & 4) != 0
    fast = fast_ref[0] != 0

    @pl.when(is_first)
    def _():
        m_scr[...] = jnp.full(m_scr.shape, NEG_INF, F32)
        l_scr[...] = jnp.zeros(l_scr.shape, F32)
        acc_scr[...] = jnp.zeros(acc_scr.shape, F32)

    def step(masked, bounded):
        if masked:
            qpos = i * tq + lax.broadcasted_iota(jnp.int32, (tq, tk), 0)
            kpos = j * tk + lax.broadcasted_iota(jnp.int32, (tq, tk), 1)
            mask = (kpos // CHUNK) <= (qpos // CHUNK)

        def head(h, carry):
            kh = k_ref[h]
            vh = v_ref[h]
            for c in range(2):
                s = _dot_nt(q_ref[h, c], kh)
                if bounded:
                    s = s - bound_ref[c]
                if masked:
                    s = jnp.where(mask, s, NEG_INF)
                if bounded:
                    p = jnp.exp(s)
                    psum = p[:, :LANES]
                    for n in range(1, tk // LANES):
                        psum = psum + p[:, n * LANES:(n + 1) * LANES]
                    l_scr[h, c] += psum
                    acc_scr[h, c] += _dot(p.astype(BF16), vh)
                else:
                    m_prev = m_scr[h, c]
                    m_next = jnp.maximum(m_prev, jnp.max(s, axis=1, keepdims=True))
                    p = jnp.exp(s - jnp.tile(m_next, (1, tk // LANES)))
                    alpha = jnp.exp(m_prev - m_next)
                    l_scr[h, c] = alpha * l_scr[h, c] + jnp.sum(p, axis=1, keepdims=True)
                    m_scr[h, c] = m_next
                    acc_scr[h, c] = alpha * acc_scr[h, c] + _dot(p.astype(BF16), vh)
            return carry

        lax.fori_loop(0, n_heads, head, 0, unroll=bounded and not masked)

    for masked in (False, True):
        for bounded in (False, True):
            pl.when((is_masked == masked) & (fast == bounded))(functools.partial(step, masked, bounded))

    def finalize(bounded):
        lam = _lambda_value(lam_ref, lam_init)

        def row_sum(h, c):
            l = l_scr[h, c]
            return jnp.sum(l, axis=1, keepdims=True) if bounded else l

        def head(h, carry):
            o = acc_scr[h, 0] / row_sum(h, 0) - lam * (acc_scr[h, 1] / row_sum(h, 1))
            o_ref[h] = _sub_ln(o, sg_ref[...], lam_init).astype(BF16)
            return carry

        lax.fori_loop(0, n_heads, head, 0)

    for bounded in (False, True):
        pl.when(is_last & (fast == bounded))(functools.partial(finalize, bounded))


def _attn_schedule(seq, tq, tk):
    qi, kj, flags = [], [], []
    for i in range(seq // tq):
        q_lo, q_hi = i * tq, (i + 1) * tq - 1
        js = [j for j in range(seq // tk) if (j * tk) // CHUNK <= q_hi // CHUNK]
        for n, j in enumerate(js):
            masked = ((j + 1) * tk - 1) // CHUNK > q_lo // CHUNK
            qi.append(i)
            kj.append(j)
            flags.append((1 if n == 0 else 0) | (2 if n == len(js) - 1 else 0) | (4 if masked else 0))
    return (np.asarray(qi, np.int32), np.asarray(kj, np.int32), np.asarray(flags, np.int32))


def _attn_prompt(q, kb, vb, lam_vec, sg, score_bound, use_bound, *, tq, tk, lam_init):
    n_heads, _, seq, _ = q.shape
    qi, kj, flags = _attn_schedule(seq, tq, tk)
    stat = (n_heads, 2, tq, LANES)
    grid_spec = pltpu.PrefetchScalarGridSpec(
        num_scalar_prefetch=5,
        grid=(len(qi),),
        in_specs=[
            pl.BlockSpec((4, LANES), lambda t, qi, kj, *_: (0, 0)),
            pl.BlockSpec((1, HEAD_W), lambda t, qi, kj, *_: (0, 0)),
            pl.BlockSpec((n_heads, 2, tq, HEAD_W), lambda t, qi, kj, *_: (0, 0, qi[t], 0)),
            pl.BlockSpec((n_heads, tk, HEAD_W), lambda t, qi, kj, *_: (0, kj[t], 0)),
            pl.BlockSpec((n_heads, tk, HEAD_W), lambda t, qi, kj, *_: (0, kj[t], 0)),
        ],
        out_specs=pl.BlockSpec((n_heads, tq, HEAD_W), lambda t, qi, kj, *_: (0, qi[t], 0)),
        scratch_shapes=[pltpu.VMEM(stat, F32), pltpu.VMEM(stat, F32), pltpu.VMEM(stat, F32)],
    )
    block_bytes = (_nbytes((n_heads, 2, tq, HEAD_W), BF16) + 2 * _nbytes((n_heads, tk, HEAD_W), BF16)
                   + _nbytes((n_heads, tq, HEAD_W), BF16))
    return pl.pallas_call(
        functools.partial(_attn_kernel, tq=tq, tk=tk, n_heads=n_heads, lam_init=lam_init),
        grid_spec=grid_spec,
        out_shape=jax.ShapeDtypeStruct((n_heads, seq, HEAD_W), BF16),
        compiler_params=pltpu.CompilerParams(
            dimension_semantics=("arbitrary",),
            vmem_limit_bytes=_vmem_limit(block_bytes, 3 * _nbytes(stat, F32))),
        name="attn_prompt",
    )(jnp.asarray(qi), jnp.asarray(kj), jnp.asarray(flags), use_bound, score_bound, lam_vec, sg, q, kb, vb)


def _attn_dec_kernel(lam_ref, sg_ref, bias_c_ref, bias_n_ref, q_ref, kn_ref, vn_ref, ck_ref, cv_ref, o_ref,
                     *, pos_chunk, t_new, n_heads, lam_init):
    lam = _lambda_value(lam_ref, lam_init)
    past = ck_ref.shape[0]
    rows = n_heads * 2 * t_new
    q = q_ref[...].reshape(rows, HEAD_W)

    def fold(carry, k_flat, v_flat, bias):
        m, l, acc = carry
        s = _dot_nt(q, k_flat) + bias
        m_new = jnp.maximum(m, jnp.max(s, axis=1, keepdims=True))
        alpha = jnp.exp(m - m_new)
        p = jnp.exp(s - m_new)
        return (m_new, alpha * l + jnp.sum(p, axis=1, keepdims=True), alpha * acc + _dot(p.astype(BF16), v_flat))

    carry = (jnp.full((rows, 1), NEG_INF, F32), jnp.zeros((rows, 1), F32), jnp.zeros((rows, HEAD_W), F32))
    for c0 in range(0, past, pos_chunk):
        k_flat = ck_ref[c0:c0 + pos_chunk].reshape(pos_chunk * n_heads, HEAD_W).astype(BF16)
        v_flat = cv_ref[c0:c0 + pos_chunk].reshape(pos_chunk * n_heads, HEAD_W).astype(BF16)
        carry = fold(carry, k_flat, v_flat, bias_c_ref[...])
    carry = fold(carry, kn_ref[...].reshape(n_heads * t_new, HEAD_W), vn_ref[...].reshape(n_heads * t_new, HEAD_W),
                 bias_n_ref[...])
    _, l, acc = carry
    o_all = acc / l
    for h in range(n_heads):
        base = h * 2 * t_new
        o = o_all[base:base + t_new] - lam * o_all[base + t_new:base + 2 * t_new]
        o_ref[h] = _sub_ln(o, sg_ref[...], lam_init).astype(BF16)


def _sample_biases(n_heads, t_new, past, pos_chunk):
    rows = n_heads * 2 * t_new
    row_head = (np.arange(rows) // (2 * t_new))[:, None]
    q_chunk = ((past + np.arange(rows) % t_new) // CHUNK)[:, None]
    assert (past - 1) // CHUNK <= past // CHUNK
    col = np.arange(pos_chunk * n_heads)[None, :]
    bias_c = np.where(col % n_heads == row_head, 0.0, NEG_INF).astype(np.float32)
    col = np.arange(n_heads * t_new)[None, :]
    visible = (col // t_new == row_head) & ((past + col % t_new) // CHUNK <= q_chunk)
    bias_n = np.where(visible, 0.0, NEG_INF).astype(np.float32)
    return jnp.asarray(bias_c), jnp.asarray(bias_n)


def _attn_sample(q, kb, vb, cache_k, cache_v, lam_vec, sg, *, layer, t_new, lam_init):
    n_heads = q.shape[0]
    _, n_seq, past, _, _ = cache_k.shape
    aw = n_heads * HEAD_W
    pos_chunk = min(past, SAMPLE_POS_CHUNK)
    assert past % pos_chunk == 0
    bias_c, bias_n = _sample_biases(n_heads, t_new, past, pos_chunk)
    cache_spec = pl.BlockSpec((None, None, past, n_heads, HEAD_W), lambda b: (layer, b, 0, 0, 0))
    block_bytes = 2 * _nbytes((past, aw), F32) + 5 * _nbytes((n_heads, t_new, HEAD_W), BF16)
    return pl.pallas_call(
        functools.partial(_attn_dec_kernel, pos_chunk=pos_chunk, t_new=t_new, n_heads=n_heads,
                          lam_init=lam_init),
        grid=(n_seq,),
        in_specs=[
            pl.BlockSpec((4, LANES), lambda b: (0, 0)),
            pl.BlockSpec((1, HEAD_W), lambda b: (0, 0)),
            pl.BlockSpec(bias_c.shape, lambda b: (0, 0), pipeline_mode=pl.Buffered(1)),
            pl.BlockSpec(bias_n.shape, lambda b: (0, 0), pipeline_mode=pl.Buffered(1)),
            pl.BlockSpec((n_heads, 2, t_new, HEAD_W), lambda b: (0, 0, b, 0)),
            pl.BlockSpec((n_heads, t_new, HEAD_W), lambda b: (0, b, 0)),
            pl.BlockSpec((n_heads, t_new, HEAD_W), lambda b: (0, b, 0)),
            cache_spec,
            cache_spec,
        ],
        out_specs=pl.BlockSpec((n_heads, t_new, HEAD_W), lambda b: (0, b, 0)),
        out_shape=jax.ShapeDtypeStruct((n_heads, n_seq * t_new, HEAD_W), BF16),
        compiler_params=pltpu.CompilerParams(
            dimension_semantics=("arbitrary",),
            vmem_limit_bytes=_vmem_limit(block_bytes, _nbytes(bias_c.shape, F32) + _nbytes(bias_n.shape, F32))),
        name="attn_sample",
    )(lam_vec, sg, bias_c, bias_n, q, kb, vb, cache_k, cache_v)


def _pool_kernel(u_ref, hist_ref, wp_ref, ps_ref, pb_ref, ext_scr, *, tm, start):
    ti = pl.program_id(1)

    @pl.when(ti == 0)
    def _():
        ext_scr[0:HIST_ROWS] = hist_ref[...]

    @pl.when(ti > 0)
    def _():
        ext_scr[0:HIST_ROWS] = ext_scr[tm:tm + HIST_ROWS]

    ext_scr[HIST_ROWS:HIST_ROWS + tm] = u_ref[...]
    pos = start + ti * tm + lax.broadcasted_iota(jnp.int32, (tm, 1), 0)
    gw = wp_ref.shape[1]
    for g, w in enumerate(POOL_WINDOWS):
        cs = slice(g * gw, (g + 1) * gw)
        cur = ext_scr[HIST_ROWS:HIST_ROWS + tm, cs]
        win = cur
        for back in range(1, w):
            win = win + ext_scr[HIST_ROWS - back:HIST_ROWS - back + tm, cs]
        cnt = jnp.minimum(w, pos + 1).astype(F32)
        p = win / cnt - cur
        y = _dot(p.astype(BF16), wp_ref[g])
        pb_ref[:, cs] = (y * ps_ref[:, cs]).astype(BF16)


def _pool(u, hist, wp_bf, ps, *, n_seq, seq_len, tm, start):
    m, pw = u.shape
    tiles = seq_len // tm
    n_groups, gw, _ = wp_bf.shape
    block_bytes = (_nbytes((tm, pw), F32) + _nbytes((HIST_ROWS, pw), F32) + _nbytes(wp_bf.shape, BF16)
                   + _nbytes((tm, pw), BF16))
    return pl.pallas_call(
        functools.partial(_pool_kernel, tm=tm, start=start),
        grid=(n_seq, tiles),
        in_specs=[
            pl.BlockSpec((tm, pw), lambda b, t: (b * tiles + t, 0)),
            pl.BlockSpec((HIST_ROWS, pw), lambda b, t: (b, 0)),
            pl.BlockSpec((n_groups, gw, gw), lambda b, t: (0, 0, 0)),
            pl.BlockSpec((1, pw), lambda b, t: (0, 0)),
        ],
        out_specs=pl.BlockSpec((tm, pw), lambda b, t: (b * tiles + t, 0)),
        out_shape=jax.ShapeDtypeStruct((m, pw), BF16),
        scratch_shapes=[pltpu.VMEM((HIST_ROWS + tm, pw), F32)],
        compiler_params=pltpu.CompilerParams(
            dimension_semantics=("arbitrary", "arbitrary"),
            vmem_limit_bytes=_vmem_limit(block_bytes, _nbytes((HIST_ROWS + tm, pw), F32))),
        name="pool_mix",
    )(u, hist, wp_bf, ps)


def _merge_kernel(o_ref, pb_ref, sig_ref, x_ref, wua_ref, wup_ref, wout_ref, x1_ref, m_scr, *, n_heads):
    d = x_ref.shape[1]
    o = jnp.concatenate([o_ref[h] for h in range(n_heads)], axis=1)
    pb = pb_ref[...]
    for n in range(d // FF_TILE):
        cs = slice(n * FF_TILE, (n + 1) * FF_TILE)
        gs = slice(d + n * FF_TILE, d + (n + 1) * FF_TILE)
        a = _dot(o, wua_ref[:, cs])
        b = _dot(pb, wup_ref[:, cs])
        m_scr[:, cs] = (sig_ref[:, cs].astype(F32) * a + sig_ref[:, gs].astype(F32) * b).astype(BF16)
    for n in range(d // FF_TILE):
        cs = slice(n * FF_TILE, (n + 1) * FF_TILE)
        x1_ref[:, cs] = x_ref[:, cs] + _dot(m_scr[...], wout_ref[:, cs])


def _merge(o, pb, sig, x, wua_bf, wup_bf, wout_bf, *, tm):
    n_heads = o.shape[0]
    m, d = x.shape
    aw = n_heads * HEAD_W
    block_bytes = (2 * _nbytes((tm, aw), BF16) + _nbytes((tm, 2 * d), BF16) + 2 * _nbytes((tm, d), F32)
                   + 2 * _nbytes((aw, d), BF16) + _nbytes((d, d), BF16))
    return pl.pallas_call(
        functools.partial(_merge_kernel, n_heads=n_heads),
        grid=(m // tm,),
        in_specs=[
            pl.BlockSpec((n_heads, tm, HEAD_W), lambda i: (0, i, 0)),
            pl.BlockSpec((tm, aw), lambda i: (i, 0)),
            pl.BlockSpec((tm, 2 * d), lambda i: (i, 0)),
            pl.BlockSpec((tm, d), lambda i: (i, 0)),
            pl.BlockSpec((aw, d), lambda i: (0, 0)),
            pl.BlockSpec((aw, d), lambda i: (0, 0)),
            pl.BlockSpec((d, d), lambda i: (0, 0)),
        ],
        out_specs=pl.BlockSpec((tm, d), lambda i: (i, 0)),
        out_shape=jax.ShapeDtypeStruct((m, d), F32),
        scratch_shapes=[pltpu.VMEM((tm, d), BF16)],
        compiler_params=pltpu.CompilerParams(
            dimension_semantics=("arbitrary",),
            vmem_limit_bytes=_vmem_limit(block_bytes, _nbytes((tm, d), BF16))),
        name="merge_out_proj",
    )(o, pb, sig, x, wua_bf, wup_bf, wout_bf)


def _ffn_kernel(*refs, tm, seq_len, streaming):
    if streaming:
        (x_ref, xp_ref, g_ref, wz_ref, wv_ref, cw_ref, cb_ref, wo_ref,
         y_ref, zt_ref, h_scr, hp_scr, zext_scr, acc_scr) = refs
    else:
        (x_ref, zs1_ref, zs2_ref, g_ref, wz_ref, wv_ref, cw_ref, cb_ref, wo_ref,
         y_ref, zt_ref, h_scr, zext_scr, acc_scr) = refs
    i = pl.program_id(0)
    f = pl.program_id(1)
    tf = wz_ref.shape[1]

    @pl.when(f == 0)
    def _():
        h_scr[...] = _rms_rows(x_ref[...], g_ref[...]).astype(BF16)
        acc_scr[...] = jnp.zeros(acc_scr.shape, F32)
        if streaming:
            hp_scr[...] = _rms_rows(xp_ref[...], g_ref[...]).astype(BF16)

    h = h_scr[...]
    if not streaming:
        r = lax.broadcasted_iota(jnp.int32, (tm, 1), 0) % seq_len
    acts = []
    for cs in (slice(0, tf // 2), slice(tf // 2, tf)):
        z = _dot(h, wz_ref[:, cs])
        val = _dot(h, wv_ref[:, cs])
        zext_scr[HIST_ROWS:HIST_ROWS + tm, cs] = z
        if streaming:
            zp = _dot(hp_scr[...], wz_ref[:, cs])
            zext_scr[0:HIST_ROWS, cs] = jnp.where(i > 0, zp, 0.0)
            z1 = zext_scr[HIST_ROWS - 1:HIST_ROWS - 1 + tm, cs]
            z2 = zext_scr[HIST_ROWS - 2:HIST_ROWS - 2 + tm, cs]
            zt_ref[:, cs] = z[tm - 8:tm]
        else:
            zext_scr[0:HIST_ROWS, cs] = jnp.zeros((HIST_ROWS, tf // 2), F32)
            z1 = jnp.where(r >= 1, zext_scr[HIST_ROWS - 1:HIST_ROWS - 1 + tm, cs], 0.0) + zs1_ref[:, cs]
            z2 = jnp.where(r >= 2, zext_scr[HIST_ROWS - 2:HIST_ROWS - 2 + tm, cs], 0.0) + zs2_ref[:, cs]
            zt_ref[:, cs] = z
        zc = cb_ref[:, cs] + z2 * cw_ref[0:1, cs] + z1 * cw_ref[1:2, cs] + z * cw_ref[2:3, cs]
        acts.append((jax.nn.silu(zc) * val).astype(BF16))
    acc_scr[...] += _dot(jnp.concatenate(acts, axis=1), wo_ref[...])

    @pl.when(f == pl.num_programs(1) - 1)
    def _():
        y_ref[...] = x_ref[...] + acc_scr[...]


def _ffn(x1, g, wz_bf, wv_bf, cw, cb, wo_bf, *, tm, seq_len, zs1=None, zs2=None):
    m, d = x1.shape
    fp = wz_bf.shape[1]
    tf = FF_TILE
    streaming = zs1 is None
    row_tile = lambda i, f: (i, 0)
    col_tile = lambda i, f: (0, f)
    w_specs = [
        pl.BlockSpec((1, d), lambda i, f: (0, 0)),
        pl.BlockSpec((d, tf), col_tile),
        pl.BlockSpec((d, tf), col_tile),
        pl.BlockSpec((CONV_WIDTH, tf), col_tile),
        pl.BlockSpec((1, tf), col_tile),
        pl.BlockSpec((tf, d), lambda i, f: (f, 0)),
    ]
    scratch = [pltpu.VMEM((tm, d), BF16)]
    if streaming:
        assert seq_len == m and tm % HIST_ROWS == 0
        per_tile = tm // HIST_ROWS
        in_specs = [pl.BlockSpec((tm, d), row_tile),
                    pl.BlockSpec((HIST_ROWS, d), lambda i, f: (jnp.maximum(i * per_tile - 1, 0), 0))] + w_specs
        args = (x1, x1, g, wz_bf, wv_bf, cw, cb, wo_bf)
        zt_shape, zt_spec = (m // tm * 8, fp), pl.BlockSpec((8, tf), lambda i, f: (i, f))
        scratch.append(pltpu.VMEM((HIST_ROWS, d), BF16))
    else:
        assert tm % seq_len == 0 and seq_len >= CONV_WIDTH - 1
        in_specs = [pl.BlockSpec((tm, d), row_tile),
                    pl.BlockSpec((tm, tf), lambda i, f: (i, f)),
                    pl.BlockSpec((tm, tf), lambda i, f: (i, f))] + w_specs
        args = (x1, zs1, zs2, g, wz_bf, wv_bf, cw, cb, wo_bf)
        zt_shape, zt_spec = (m, fp), pl.BlockSpec((tm, tf), lambda i, f: (i, f))
    scratch += [pltpu.VMEM((HIST_ROWS + tm, tf), F32), pltpu.VMEM((tm, d), F32)]
    block_bytes = (2 * _nbytes((tm, d), F32) + 2 * _nbytes((d, tf), BF16) + _nbytes((tf, d), BF16)
                   + 3 * _nbytes((tm, tf), F32))
    scratch_bytes = _nbytes((tm, d), BF16) + _nbytes((HIST_ROWS + tm, tf), F32) + _nbytes((tm, d), F32)
    return pl.pallas_call(
        functools.partial(_ffn_kernel, tm=tm, seq_len=seq_len, streaming=streaming),
        grid=(m // tm, fp // tf),
        in_specs=in_specs,
        out_specs=(pl.BlockSpec((tm, d), row_tile), zt_spec),
        out_shape=(jax.ShapeDtypeStruct((m, d), F32), jax.ShapeDtypeStruct(zt_shape, F32)),
        scratch_shapes=scratch,
        compiler_params=pltpu.CompilerParams(
            dimension_semantics=("arbitrary", "arbitrary"),
            vmem_limit_bytes=_vmem_limit(block_bytes, scratch_bytes)),
        name="conv_glu_ffn",
    )(*args)


def _prep_weights(attn_norm_g, w_in, q_norm_g, k_norm_g, lambda_q1, lambda_k1, lambda_q2, lambda_k2,
                  subln_g, w_pool, pool_scale, w_up_attn, w_up_pool, w_out,
                  ffn_norm_g, w_ffn_in, conv_w, conv_b, w_ffn_out, *, n_heads):
    d_ff = conv_w.shape[1]
    fp = -(-d_ff // FF_TILE) * FF_TILE
    pad = fp - d_ff
    lam_vec = jnp.pad(jnp.stack([lambda_q1, lambda_k1, lambda_q2, lambda_k2]).astype(F32),
                      ((0, 0), (0, LANES - lambda_q1.shape[0])))
    group = np.arange(MXU_TILE) // QK_HALF
    gd = jnp.asarray((group[:, None] == group[None, :]).astype(np.float32) / QK_HALF, BF16)
    score_bound = (QK_HALF ** 0.5 * BOUND_MARGIN * jnp.max(jnp.abs(q_norm_g), axis=1)
                   * jnp.max(jnp.abs(k_norm_g), axis=1)).astype(F32)
    use_bound = (jnp.max(score_bound) <= MAX_FIXED_SHIFT).astype(jnp.int32).reshape(1)
    return dict(
        score_bound=score_bound,
        use_bound=use_bound,
        attn_g=attn_norm_g.reshape(1, -1),
        w_in=w_in.astype(BF16),
        qg=(jnp.tile(q_norm_g.reshape(-1), n_heads) * (QK_HALF ** -0.5)).reshape(1, -1),
        kg=jnp.tile(k_norm_g.reshape(-1), n_heads).reshape(1, -1),
        gd=gd,
        lam_vec=lam_vec,
        sg=subln_g.reshape(1, -1),
        w_pool=w_pool.astype(BF16),
        pool_scale=pool_scale.reshape(1, -1),
        w_up_attn=w_up_attn.astype(BF16),
        w_up_pool=w_up_pool.astype(BF16),
        w_out=w_out.astype(BF16),
        ffn_g=ffn_norm_g.reshape(1, -1),
        wz=jnp.pad(w_ffn_in[:, :d_ff].astype(BF16), ((0, 0), (0, pad))),
        wv=jnp.pad(w_ffn_in[:, d_ff:].astype(BF16), ((0, 0), (0, pad))),
        conv_w=jnp.pad(conv_w, ((0, 0), (0, pad))),
        conv_b=jnp.pad(conv_b, (0, pad)).reshape(1, -1),
        w_ffn_out=jnp.pad(w_ffn_out.astype(BF16), ((0, pad), (0, 0))),
        d_ff=d_ff,
    )


def _layer(x, start, caches, layer, pool_hist, conv_hist, lam_init, w, *, n_heads, tm, tq):
    n_seq, t_len, d = x.shape
    m = n_seq * t_len
    d_ff = w["d_ff"]
    x2 = x.reshape(m, d)

    q, k_f32, kb, v_f32, vb, u, sig = _in_proj(
        x2, w["attn_g"], w["w_in"], w["qg"], w["kg"], w["gd"], tm=min(tm, INPROJ_ROWS), n_heads=n_heads)

    if caches is None:
        o = _attn_prompt(q, kb, vb, w["lam_vec"], w["sg"], w["score_bound"], w["use_bound"],
                         tq=tq, tk=tq, lam_init=lam_init)
    else:
        o = _attn_sample(q, kb, vb, caches[0], caches[1], w["lam_vec"], w["sg"],
                         layer=layer, t_new=t_len, lam_init=lam_init)

    hist = jnp.pad(pool_hist, ((0, 0), (HIST_ROWS - POOL_HIST, 0), (0, 0))).reshape(n_seq * HIST_ROWS, -1)
    pool_tm = min(tm, t_len)
    pb = _pool(u, hist, w["w_pool"], w["pool_scale"], n_seq=n_seq, seq_len=t_len, tm=pool_tm, start=start)

    x1 = _merge(o, pb, sig, x2, w["w_up_attn"], w["w_up_pool"], w["w_out"], tm=min(tm, 256))

    if conv_hist is None:
        y, z_tail = _ffn(x1, w["ffn_g"], w["wz"], w["wv"], w["conv_w"], w["conv_b"], w["w_ffn_out"],
                         tm=tm, seq_len=t_len)
        conv_state = z_tail[-(CONV_WIDTH - 1):, :d_ff][None]
    else:
        fpad = w["conv_b"].shape[1] - d_ff
        zs1 = jnp.pad(conv_hist[:, 1:2], ((0, 0), (0, t_len - 1), (0, fpad))).reshape(m, -1)
        zs2 = jnp.pad(conv_hist, ((0, 0), (0, t_len - (CONV_WIDTH - 1)), (0, fpad))).reshape(m, -1)
        y, z_all = _ffn(x1, w["ffn_g"], w["wz"], w["wv"], w["conv_w"], w["conv_b"], w["w_ffn_out"],
                        tm=tm, seq_len=t_len, zs1=zs1, zs2=zs2)
        conv_state = z_all.reshape(n_seq, t_len, -1)[:, -(CONV_WIDTH - 1):, :d_ff]

    u3 = u.reshape(n_seq, t_len, -1)
    pool_state = jnp.concatenate([pool_hist, u3], axis=1)[:, -POOL_HIST:] if t_len < POOL_HIST else u3[:, -POOL_HIST:]
    return (y.reshape(n_seq, t_len, d),
            k_f32.reshape(n_seq, t_len, n_heads, HEAD_W),
            v_f32.reshape(n_seq, t_len, n_heads, HEAD_W),
            pool_state, conv_state)


def kernel(x_prompt, x_sample, cache_k, cache_v, state_pool, state_conv, attn_norm_g, w_in, q_norm_g, k_norm_g, lambda_q1, lambda_k1, lambda_q2, lambda_k2, subln_g, w_pool, pool_scale, w_up_attn, w_up_pool, w_out, ffn_norm_g, w_ffn_in, conv_w, conv_b, w_ffn_out):
    depth = w_in.shape[0]
    n_heads = cache_k.shape[3]
    past = cache_k.shape[2]
    y_p, y_s = x_prompt, x_sample
    outs = [[] for _ in range(8)]
    for l in range(depth):
        lam_init = 0.8 - 0.6 * math.exp(-0.3 * l)
        w = _prep_weights(attn_norm_g[l], w_in[l], q_norm_g[l], k_norm_g[l], lambda_q1[l], lambda_k1[l],
                          lambda_q2[l], lambda_k2[l], subln_g[l], w_pool[l], pool_scale[l], w_up_attn[l],
                          w_up_pool[l], w_out[l], ffn_norm_g[l], w_ffn_in[l], conv_w[l], conv_b[l],
                          w_ffn_out[l], n_heads=n_heads)
        pool0 = jnp.zeros((x_prompt.shape[0], POOL_HIST, state_pool.shape[-1]), F32)
        y_p, kp, vp, pp, cp = _layer(y_p, 0, None, l, pool0, None, lam_init, w,
                                     n_heads=n_heads, tm=512, tq=512)
        y_s, ks, vs, ps, cs = _layer(y_s, past, (cache_k, cache_v), l, state_pool[l], state_conv[l],
                                     lam_init, w, n_heads=n_heads,
                                     tm=x_sample.shape[0] * x_sample.shape[1], tq=None)
        for lst, val in zip(outs, (kp, vp, pp, cp, ks, vs, ps, cs)):
            lst.append(val)
    return (y_p, y_s) + tuple(jnp.stack(lst) for lst in outs)
```

```python
import functools
import math

import numpy as np
import jax
import jax.numpy as jnp
from jax import lax
from jax.experimental import pallas as pl
from jax.experimental.pallas import tpu as pltpu

F32 = jnp.float32
BF16 = jnp.bfloat16

EPS = 1e-6
NEG_INF = -1e30
CHUNK = 64
POOL_WINDOWS = (2, 4, 8, 16)
POOL_HIST = max(POOL_WINDOWS) - 1
CONV_WIDTH = 3

LANES = 128
HIST_ROWS = 16
MXU_TILE = 256
VMEM_LIMIT_CAP = 60 * 1024 * 1024
VMEM_TEMP_ALLOWANCE = 12 * 1024 * 1024

HEAD_W = 128
QK_HALF = 64
FF_TILE = 512
SAMPLE_POS_CHUNK = 512
INPROJ_ROWS = 256
CAST_ROWS = 512
CAST_ROWS_WIDE = 128
MAX_FIXED_SHIFT = 40.0
BOUND_MARGIN = 1.02


def _vmem_limit(block_bytes, scratch_bytes):
    est = 2 * block_bytes + scratch_bytes + VMEM_TEMP_ALLOWANCE
    return int(min(est, VMEM_LIMIT_CAP))


def _nbytes(shape, dtype):
    return int(np.prod(shape)) * jnp.dtype(dtype).itemsize


def _rms_rows(x, g):
    ms = jnp.mean(x * x, axis=-1, keepdims=True)
    return x * lax.rsqrt(ms + EPS) * g


def _dot(a, b):
    return jnp.dot(a, b, preferred_element_type=F32)


def _dot_nt(a, b):
    return lax.dot_general(a, b, (((1,), (1,)), ((), ())), preferred_element_type=F32)


def _cast_kernel(x_ref, o_ref):
    o_ref[...] = x_ref[...].astype(BF16)


def _to_bf16(w, *, rows):
    r, c = w.shape
    return pl.pallas_call(
        _cast_kernel,
        grid=(r // rows,),
        in_specs=[pl.BlockSpec((rows, c), lambda i: (i, 0))],
        out_specs=pl.BlockSpec((rows, c), lambda i: (i, 0)),
        out_shape=jax.ShapeDtypeStruct((r, c), BF16),
        compiler_params=pltpu.CompilerParams(
            dimension_semantics=("arbitrary",),
            vmem_limit_bytes=_vmem_limit(_nbytes((rows, c), F32) + _nbytes((rows, c), BF16), 0)),
        name="cast_bf16",
    )(w)


def _split_cast_kernel(x_ref, wz_ref, wv_ref, *, d_ff):
    rows, fp = wz_ref.shape
    zeros = jnp.zeros((rows, fp - d_ff), BF16)
    wz_ref[:, :d_ff] = x_ref[:, :d_ff].astype(BF16)
    wz_ref[:, d_ff:] = zeros
    wv_ref[:, :d_ff] = x_ref[:, d_ff:].astype(BF16)
    wv_ref[:, d_ff:] = zeros


def _split_ffn_in(w, *, d_ff, fp, rows):
    d = w.shape[0]
    out = jax.ShapeDtypeStruct((d, fp), BF16)
    return pl.pallas_call(
        functools.partial(_split_cast_kernel, d_ff=d_ff),
        grid=(d // rows,),
        in_specs=[pl.BlockSpec((rows, 2 * d_ff), lambda i: (i, 0))],
        out_specs=(pl.BlockSpec((rows, fp), lambda i: (i, 0)), pl.BlockSpec((rows, fp), lambda i: (i, 0))),
        out_shape=(out, out),
        compiler_params=pltpu.CompilerParams(
            dimension_semantics=("arbitrary",),
            vmem_limit_bytes=_vmem_limit(_nbytes((rows, 2 * d_ff), F32) + 2 * _nbytes((rows, fp), BF16), 0)),
        name="split_cast_ffn_in",
    )(w)


def _pad_cast_kernel(x_ref, o_ref, *, n_in):
    keep = pl.program_id(0) < n_in
    o_ref[...] = jnp.where(keep, x_ref[...], 0.0).astype(BF16)


def _pad_rows_bf16(w, *, fp, rows):
    d_ff, d = w.shape
    n_in = d_ff // rows
    return pl.pallas_call(
        functools.partial(_pad_cast_kernel, n_in=n_in),
        grid=(fp // rows,),
        in_specs=[pl.BlockSpec((rows, d), lambda i: (jnp.minimum(i, n_in - 1), 0))],
        out_specs=pl.BlockSpec((rows, d), lambda i: (i, 0)),
        out_shape=jax.ShapeDtypeStruct((fp, d), BF16),
        compiler_params=pltpu.CompilerParams(
            dimension_semantics=("arbitrary",),
            vmem_limit_bytes=_vmem_limit(_nbytes((rows, d), F32) + _nbytes((rows, d), BF16), 0)),
        name="pad_cast_ffn_out",
    )(w)


def _inproj_kernel(x_ref, g_ref, w_ref, qg_ref, kg_ref, gd_ref,
                   q_ref, kf_ref, kb_ref, vf_ref, vb_ref, u_ref, sig_ref, h_scr,
                   *, n_heads):
    tm = x_ref.shape[0]
    aw = n_heads * HEAD_W
    n_col = w_ref.shape[1] // aw
    h_scr[...] = _rms_rows(x_ref[...], g_ref[...]).astype(BF16)

    def col_tile(j):
        return _dot(h_scr[...], w_ref[:, j * aw:(j + 1) * aw])

    def qk_norm(y, gain_ref):
        heads = []
        for c in range(aw // MXU_TILE):
            cs = slice(c * MXU_TILE, (c + 1) * MXU_TILE)
            yc = y[:, cs]
            ms = _dot((yc * yc).astype(BF16), gd_ref[...])
            yn = yc * lax.rsqrt(ms + EPS) * gain_ref[:, cs]
            heads += [yn[:, :HEAD_W], yn[:, HEAD_W:]]
        return heads

    for j in range(4, n_col):
        sig_ref[:, (j - 4) * aw:(j - 3) * aw] = jax.nn.sigmoid(col_tile(j)).astype(BF16)

    feat = lax.broadcasted_iota(jnp.int32, (tm, HEAD_W), 1)
    for h, yh in enumerate(qk_norm(col_tile(0), qg_ref)):
        q_ref[h, 0] = jnp.where(feat < QK_HALF, yh, 0.0).astype(BF16)
        q_ref[h, 1] = jnp.where(feat >= QK_HALF, yh, 0.0).astype(BF16)

    for h, yh in enumerate(qk_norm(col_tile(1), kg_ref)):
        kf_ref[:, h * HEAD_W:(h + 1) * HEAD_W] = yh
        kb_ref[h] = yh.astype(BF16)

    y = col_tile(2)
    vf_ref[...] = y
    for h in range(n_heads):
        vb_ref[h] = y[:, h * HEAD_W:(h + 1) * HEAD_W].astype(BF16)

    u_ref[...] = col_tile(3)


def _in_proj(x, g, w_bf, qg, kg, gd, *, tm, n_heads):
    m, d = x.shape
    aw = n_heads * HEAD_W
    n_all = w_bf.shape[1]
    gate_w = n_all - 4 * aw
    q_shape, q_spec = (n_heads, 2, m, HEAD_W), pl.BlockSpec((n_heads, 2, tm, HEAD_W), lambda i: (0, 0, i, 0))
    v_shape, v_spec = (n_heads, m, HEAD_W), pl.BlockSpec((n_heads, tm, HEAD_W), lambda i: (0, i, 0))
    out_shape = (
        jax.ShapeDtypeStruct(q_shape, BF16),
        jax.ShapeDtypeStruct((m, aw), F32),
        jax.ShapeDtypeStruct((n_heads, m, HEAD_W), BF16),
        jax.ShapeDtypeStruct((m, aw), F32),
        jax.ShapeDtypeStruct(v_shape, BF16),
        jax.ShapeDtypeStruct((m, aw), F32),
        jax.ShapeDtypeStruct((m, gate_w), BF16),
    )
    resident = pl.Buffered(1)
    in_specs = [
        pl.BlockSpec((tm, d), lambda i: (i, 0)),
        pl.BlockSpec((1, d), lambda i: (0, 0)),
        pl.BlockSpec((d, n_all), lambda i: (0, 0), pipeline_mode=resident),
        pl.BlockSpec((1, aw), lambda i: (0, 0)),
        pl.BlockSpec((1, aw), lambda i: (0, 0)),
        pl.BlockSpec((MXU_TILE, MXU_TILE), lambda i: (0, 0)),
    ]
    out_specs = (
        q_spec,
        pl.BlockSpec((tm, aw), lambda i: (i, 0)),
        pl.BlockSpec((n_heads, tm, HEAD_W), lambda i: (0, i, 0)),
        pl.BlockSpec((tm, aw), lambda i: (i, 0)),
        v_spec,
        pl.BlockSpec((tm, aw), lambda i: (i, 0)),
        pl.BlockSpec((tm, gate_w), lambda i: (i, 0)),
    )
    block_bytes = (_nbytes((tm, d), F32) + 3 * _nbytes((tm, aw), F32) + 4 * _nbytes((tm, aw), BF16)
                   + _nbytes((tm, gate_w), BF16))
    return pl.pallas_call(
        functools.partial(_inproj_kernel, n_heads=n_heads),
        grid=(m // tm,),
        in_specs=in_specs,
        out_specs=out_specs,
        out_shape=out_shape,
        scratch_shapes=[pltpu.VMEM((tm, d), BF16)],
        compiler_params=pltpu.CompilerParams(
            dimension_semantics=("arbitrary",),
            vmem_limit_bytes=_vmem_limit(block_bytes, _nbytes((d, n_all), BF16) + _nbytes((tm, d), BF16))),
        name="in_proj",
    )(x, g, w_bf, qg, kg, gd)


def _lambda_value(lam_ref, lam_init):
    a = lam_ref[...]
    s1 = jnp.sum(a[0:1] * a[1:2], axis=1, keepdims=True)
    s2 = jnp.sum(a[2:3] * a[3:4], axis=1, keepdims=True)
    return jnp.exp(s1) - jnp.exp(s2) + lam_init


def _sub_ln(o, sg, lam_init):
    o = o * lax.rsqrt(jnp.mean(o * o, axis=-1, keepdims=True) + EPS)
    return o * sg * (1.0 - lam_init)


def _attn_kernel(qi_ref, kj_ref, flag_ref, fast_ref, bound_ref, lam_ref, sg_ref, q_ref, k_ref, v_ref, o_ref,
                 m_scr, l_scr, acc_scr, *, tq, tk, n_heads, lam_init):
    t = pl.program_id(0)
    i = qi_ref[t]
    j = kj_ref[t]
    flags = flag_ref[t]
    is_first = (flags & 1) != 0
    is_last = (flags & 2) != 0
    is_masked = (flags & 4) != 0
    fast = fast_ref[0] != 0

    @pl.when(is_first)
    def _():
        m_scr[...] = jnp.full(m_scr.shape, NEG_INF, F32)
        l_scr[...] = jnp.zeros(l_scr.shape, F32)
        acc_scr[...] = jnp.zeros(acc_scr.shape, F32)

    def step(masked, bounded):
        if masked:
            qpos = i * tq + lax.broadcasted_iota(jnp.int32, (tq, tk), 0)
            kpos = j * tk + lax.broadcasted_iota(jnp.int32, (tq, tk), 1)
            mask = (kpos // CHUNK) <= (qpos // CHUNK)

        def head(h, carry):
            kh = k_ref[h]
            vh = v_ref[h]
            for c in range(2):
                s = _dot_nt(q_ref[h, c], kh)
                if bounded:
                    s = s - bound_ref[c]
                if masked:
                    s = jnp.where(mask, s, NEG_INF)
                if bounded:
                    p = jnp.exp(s)
                    psum = p[:, :LANES]
                    for n in range(1, tk // LANES):
                        psum = psum + p[:, n * LANES:(n + 1) * LANES]
                    l_scr[h, c] += psum
                    acc_scr[h, c] += _dot(p.astype(BF16), vh)
                else:
                    m_prev = m_scr[h, c]
                    m_next = jnp.maximum(m_prev, jnp.max(s, axis=1, keepdims=True))
                    p = jnp.exp(s - jnp.tile(m_next, (1, tk // LANES)))
                    alpha = jnp.exp(m_prev - m_next)
                    l_scr[h, c] = alpha * l_scr[h, c] + jnp.sum(p, axis=1, keepdims=True)
                    m_scr[h, c] = m_next
                    acc_scr[h, c] = alpha * acc_scr[h, c] + _dot(p.astype(BF16), vh)
            return carry

        lax.fori_loop(0, n_heads, head, 0, unroll=bounded)

    for masked in (False, True):
        for bounded in (False, True):
            pl.when((is_masked == masked) & (fast == bounded))(functools.partial(step, masked, bounded))

    def finalize(bounded):
        lam = _lambda_value(lam_ref, lam_init)

        def row_sum(h, c):
            l = l_scr[h, c]
            return jnp.sum(l, axis=1, keepdims=True) if bounded else l

        def head(h, carry):
            o = acc_scr[h, 0] / row_sum(h, 0) - lam * (acc_scr[h, 1] / row_sum(h, 1))
            o_ref[h] = _sub_ln(o, sg_ref[...], lam_init).astype(BF16)
            return carry

        lax.fori_loop(0, n_heads, head, 0)

    for bounded in (False, True):
        pl.when(is_last & (fast == bounded))(functools.partial(finalize, bounded))


def _attn_schedule(seq, tq, tk):
    qi, kj, flags = [], [], []
    for i in range(seq // tq):
        q_lo, q_hi = i * tq, (i + 1) * tq - 1
        js = [j for j in range(seq // tk) if (j * tk) // CHUNK <= q_hi // CHUNK]
        for n, j in enumerate(js):
            masked = ((j + 1) * tk - 1) // CHUNK > q_lo // CHUNK
            qi.append(i)
            kj.append(j)
            flags.append((1 if n == 0 else 0) | (2 if n == len(js) - 1 else 0) | (4 if masked else 0))
    return (np.asarray(qi, np.int32), np.asarray(kj, np.int32), np.asarray(flags, np.int32))


def _attn_prompt(q, kb, vb, lam_vec, sg, score_bound, use_bound, *, tq, tk, lam_init):
    n_heads, _, seq, _ = q.shape
    qi, kj, flags = _attn_schedule(seq, tq, tk)
    stat = (n_heads, 2, tq, LANES)
    grid_spec = pltpu.PrefetchScalarGridSpec(
        num_scalar_prefetch=5,
        grid=(len(qi),),
        in_specs=[
            pl.BlockSpec((4, LANES), lambda t, qi, kj, *_: (0, 0)),
            pl.BlockSpec((1, HEAD_W), lambda t, qi, kj, *_: (0, 0)),
            pl.BlockSpec((n_heads, 2, tq, HEAD_W), lambda t, qi, kj, *_: (0, 0, qi[t], 0)),
            pl.BlockSpec((n_heads, tk, HEAD_W), lambda t, qi, kj, *_: (0, kj[t], 0)),
            pl.BlockSpec((n_heads, tk, HEAD_W), lambda t, qi, kj, *_: (0, kj[t], 0)),
        ],
        out_specs=pl.BlockSpec((n_heads, tq, HEAD_W), lambda t, qi, kj, *_: (0, qi[t], 0)),
        scratch_shapes=[pltpu.VMEM(stat, F32), pltpu.VMEM(stat, F32), pltpu.VMEM(stat, F32)],
    )
    block_bytes = (_nbytes((n_heads, 2, tq, HEAD_W), BF16) + 2 * _nbytes((n_heads, tk, HEAD_W), BF16)
                   + _nbytes((n_heads, tq, HEAD_W), BF16))
    return pl.pallas_call(
        functools.partial(_attn_kernel, tq=tq, tk=tk, n_heads=n_heads, lam_init=lam_init),
        grid_spec=grid_spec,
        out_shape=jax.ShapeDtypeStruct((n_heads, seq, HEAD_W), BF16),
        compiler_params=pltpu.CompilerParams(
            dimension_semantics=("arbitrary",),
            vmem_limit_bytes=_vmem_limit(block_bytes, 3 * _nbytes(stat, F32))),
        name="attn_prompt",
    )(jnp.asarray(qi), jnp.asarray(kj), jnp.asarray(flags), use_bound, score_bound, lam_vec, sg, q, kb, vb)


def _attn_dec_kernel(lam_ref, sg_ref, bias_c_ref, bias_n_ref, q_ref, kn_ref, vn_ref, ck_ref, cv_ref, o_ref,
                     *, pos_chunk, t_new, n_heads, lam_init):
    lam = _lambda_value(lam_ref, lam_init)
    past = ck_ref.shape[0]
    rows = n_heads * 2 * t_new
    q = q_ref[...].reshape(rows, HEAD_W)

    def fold(carry, k_flat, v_flat, bias):
        m, l, acc = carry
        s = _dot_nt(q, k_flat) + bias
        m_new = jnp.maximum(m, jnp.max(s, axis=1, keepdims=True))
        alpha = jnp.exp(m - m_new)
        p = jnp.exp(s - m_new)
        return (m_new, alpha * l + jnp.sum(p, axis=1, keepdims=True), alpha * acc + _dot(p.astype(BF16), v_flat))

    carry = (jnp.full((rows, 1), NEG_INF, F32), jnp.zeros((rows, 1), F32), jnp.zeros((rows, HEAD_W), F32))
    for c0 in range(0, past, pos_chunk):
        k_flat = ck_ref[c0:c0 + pos_chunk].reshape(pos_chunk * n_heads, HEAD_W).astype(BF16)
        v_flat = cv_ref[c0:c0 + pos_chunk].reshape(pos_chunk * n_heads, HEAD_W).astype(BF16)
        carry = fold(carry, k_flat, v_flat, bias_c_ref[...])
    carry = fold(carry, kn_ref[...].reshape(n_heads * t_new, HEAD_W), vn_ref[...].reshape(n_heads * t_new, HEAD_W),
                 bias_n_ref[...])
    _, l, acc = carry
    o_all = acc / l
    for h in range(n_heads):
        base = h * 2 * t_new
        o = o_all[base:base + t_new] - lam * o_all[base + t_new:base + 2 * t_new]
        o_ref[h] = _sub_ln(o, sg_ref[...], lam_init).astype(BF16)


def _sample_biases(n_heads, t_new, past, pos_chunk):
    rows = n_heads * 2 * t_new
    row_head = (np.arange(rows) // (2 * t_new))[:, None]
    q_chunk = ((past + np.arange(rows) % t_new) // CHUNK)[:, None]
    assert (past - 1) // CHUNK <= past // CHUNK
    col = np.arange(pos_chunk * n_heads)[None, :]
    bias_c = np.where(col % n_heads == row_head, 0.0, NEG_INF).astype(np.float32)
    col = np.arange(n_heads * t_new)[None, :]
    visible = (col // t_new == row_head) & ((past + col % t_new) // CHUNK <= q_chunk)
    bias_n = np.where(visible, 0.0, NEG_INF).astype(np.float32)
    return jnp.asarray(bias_c), jnp.asarray(bias_n)


def _attn_sample(q, kb, vb, cache_k, cache_v, lam_vec, sg, *, layer, t_new, lam_init):
    n_heads = q.shape[0]
    _, n_seq, past, _, _ = cache_k.shape
    aw = n_heads * HEAD_W
    pos_chunk = min(past, SAMPLE_POS_CHUNK)
    assert past % pos_chunk == 0
    bias_c, bias_n = _sample_biases(n_heads, t_new, past, pos_chunk)
    cache_spec = pl.BlockSpec((None, None, past, n_heads, HEAD_W), lambda b: (layer, b, 0, 0, 0))
    block_bytes = 2 * _nbytes((past, aw), F32) + 5 * _nbytes((n_heads, t_new, HEAD_W), BF16)
    return pl.pallas_call(
        functools.partial(_attn_dec_kernel, pos_chunk=pos_chunk, t_new=t_new, n_heads=n_heads,
                          lam_init=lam_init),
        grid=(n_seq,),
        in_specs=[
            pl.BlockSpec((4, LANES), lambda b: (0, 0)),
            pl.BlockSpec((1, HEAD_W), lambda b: (0, 0)),
            pl.BlockSpec(bias_c.shape, lambda b: (0, 0), pipeline_mode=pl.Buffered(1)),
            pl.BlockSpec(bias_n.shape, lambda b: (0, 0), pipeline_mode=pl.Buffered(1)),
            pl.BlockSpec((n_heads, 2, t_new, HEAD_W), lambda b: (0, 0, b, 0)),
            pl.BlockSpec((n_heads, t_new, HEAD_W), lambda b: (0, b, 0)),
            pl.BlockSpec((n_heads, t_new, HEAD_W), lambda b: (0, b, 0)),
            cache_spec,
            cache_spec,
        ],
        out_specs=pl.BlockSpec((n_heads, t_new, HEAD_W), lambda b: (0, b, 0)),
        out_shape=jax.ShapeDtypeStruct((n_heads, n_seq * t_new, HEAD_W), BF16),
        compiler_params=pltpu.CompilerParams(
            dimension_semantics=("arbitrary",),
            vmem_limit_bytes=_vmem_limit(block_bytes, _nbytes(bias_c.shape, F32) + _nbytes(bias_n.shape, F32))),
        name="attn_sample",
    )(lam_vec, sg, bias_c, bias_n, q, kb, vb, cache_k, cache_v)


def _pool_kernel(u_ref, hist_ref, wp_ref, ps_ref, pb_ref, ext_scr, *, tm, start):
    ti = pl.program_id(1)

    @pl.when(ti == 0)
    def _():
        ext_scr[0:HIST_ROWS] = hist_ref[...]

    @pl.when(ti > 0)
    def _():
        ext_scr[0:HIST_ROWS] = ext_scr[tm:tm + HIST_ROWS]

    ext_scr[HIST_ROWS:HIST_ROWS + tm] = u_ref[...]
    pos = start + ti * tm + lax.broadcasted_iota(jnp.int32, (tm, 1), 0)
    gw = wp_ref.shape[1]
    for g, w in enumerate(POOL_WINDOWS):
        cs = slice(g * gw, (g + 1) * gw)
        cur = ext_scr[HIST_ROWS:HIST_ROWS + tm, cs]
        win = cur
        for back in range(1, w):
            win = win + ext_scr[HIST_ROWS - back:HIST_ROWS - back + tm, cs]
        cnt = jnp.minimum(w, pos + 1).astype(F32)
        p = win / cnt - cur
        y = _dot(p.astype(BF16), wp_ref[g])
        pb_ref[:, cs] = (y * ps_ref[:, cs]).astype(BF16)


def _pool(u, hist, wp_bf, ps, *, n_seq, seq_len, tm, start):
    m, pw = u.shape
    tiles = seq_len // tm
    n_groups, gw, _ = wp_bf.shape
    block_bytes = (_nbytes((tm, pw), F32) + _nbytes((HIST_ROWS, pw), F32) + _nbytes(wp_bf.shape, BF16)
                   + _nbytes((tm, pw), BF16))
    return pl.pallas_call(
        functools.partial(_pool_kernel, tm=tm, start=start),
        grid=(n_seq, tiles),
        in_specs=[
            pl.BlockSpec((tm, pw), lambda b, t: (b * tiles + t, 0)),
            pl.BlockSpec((HIST_ROWS, pw), lambda b, t: (b, 0)),
            pl.BlockSpec((n_groups, gw, gw), lambda b, t: (0, 0, 0)),
            pl.BlockSpec((1, pw), lambda b, t: (0, 0)),
        ],
        out_specs=pl.BlockSpec((tm, pw), lambda b, t: (b * tiles + t, 0)),
        out_shape=jax.ShapeDtypeStruct((m, pw), BF16),
        scratch_shapes=[pltpu.VMEM((HIST_ROWS + tm, pw), F32)],
        compiler_params=pltpu.CompilerParams(
            dimension_semantics=("arbitrary", "arbitrary"),
            vmem_limit_bytes=_vmem_limit(block_bytes, _nbytes((HIST_ROWS + tm, pw), F32))),
        name="pool_mix",
    )(u, hist, wp_bf, ps)


def _merge_kernel(o_ref, pb_ref, sig_ref, x_ref, wua_ref, wup_ref, wout_ref, x1_ref, m_scr, *, n_heads):
    d = x_ref.shape[1]
    o = jnp.concatenate([o_ref[h] for h in range(n_heads)], axis=1)
    pb = pb_ref[...]
    for n in range(d // FF_TILE):
        cs = slice(n * FF_TILE, (n + 1) * FF_TILE)
        gs = slice(d + n * FF_TILE, d + (n + 1) * FF_TILE)
        a = _dot(o, wua_ref[:, cs])
        b = _dot(pb, wup_ref[:, cs])
        m_scr[:, cs] = (sig_ref[:, cs].astype(F32) * a + sig_ref[:, gs].astype(F32) * b).astype(BF16)
    for n in range(d // FF_TILE):
        cs = slice(n * FF_TILE, (n + 1) * FF_TILE)
        x1_ref[:, cs] = x_ref[:, cs] + _dot(m_scr[...], wout_ref[:, cs])


def _merge(o, pb, sig, x, wua_bf, wup_bf, wout_bf, *, tm):
    n_heads = o.shape[0]
    m, d = x.shape
    aw = n_heads * HEAD_W
    block_bytes = 2 * _nbytes((tm, aw), BF16) + _nbytes((tm, 2 * d), BF16) + 2 * _nbytes((tm, d), F32)
    weight_bytes = 2 * _nbytes((aw, d), BF16) + _nbytes((d, d), BF16)
    resident = pl.Buffered(1)
    return pl.pallas_call(
        functools.partial(_merge_kernel, n_heads=n_heads),
        grid=(m // tm,),
        in_specs=[
            pl.BlockSpec((n_heads, tm, HEAD_W), lambda i: (0, i, 0)),
            pl.BlockSpec((tm, aw), lambda i: (i, 0)),
            pl.BlockSpec((tm, 2 * d), lambda i: (i, 0)),
            pl.BlockSpec((tm, d), lambda i: (i, 0)),
            pl.BlockSpec((aw, d), lambda i: (0, 0), pipeline_mode=resident),
            pl.BlockSpec((aw, d), lambda i: (0, 0), pipeline_mode=resident),
            pl.BlockSpec((d, d), lambda i: (0, 0), pipeline_mode=resident),
        ],
        out_specs=pl.BlockSpec((tm, d), lambda i: (i, 0)),
        out_shape=jax.ShapeDtypeStruct((m, d), F32),
        scratch_shapes=[pltpu.VMEM((tm, d), BF16)],
        compiler_params=pltpu.CompilerParams(
            dimension_semantics=("arbitrary",),
            vmem_limit_bytes=_vmem_limit(block_bytes, weight_bytes + _nbytes((tm, d), BF16))),
        name="merge_out_proj",
    )(o, pb, sig, x, wua_bf, wup_bf, wout_bf)


def _ffn_kernel(*refs, tm, seq_len, streaming):
    if streaming:
        (x_ref, xp_ref, g_ref, wz_ref, wv_ref, cw_ref, cb_ref, wo_ref,
         y_ref, zt_ref, h_scr, hp_scr, zext_scr, acc_scr) = refs
    else:
        (x_ref, zs1_ref, zs2_ref, g_ref, wz_ref, wv_ref, cw_ref, cb_ref, wo_ref,
         y_ref, zt_ref, h_scr, zext_scr, acc_scr) = refs
    i = pl.program_id(0)
    f = pl.program_id(1)
    tf = wz_ref.shape[1]

    @pl.when(f == 0)
    def _():
        h_scr[...] = _rms_rows(x_ref[...], g_ref[...]).astype(BF16)
        acc_scr[...] = jnp.zeros(acc_scr.shape, F32)
        if streaming:
            hp_scr[...] = _rms_rows(xp_ref[...], g_ref[...]).astype(BF16)

    h = h_scr[...]
    if not streaming:
        r = lax.broadcasted_iota(jnp.int32, (tm, 1), 0) % seq_len
    acts = []
    for cs in (slice(0, tf // 2), slice(tf // 2, tf)):
        z = _dot(h, wz_ref[:, cs])
        val = _dot(h, wv_ref[:, cs])
        zext_scr[HIST_ROWS:HIST_ROWS + tm, cs] = z
        if streaming:
            zp = _dot(hp_scr[...], wz_ref[:, cs])
            zext_scr[0:HIST_ROWS, cs] = jnp.where(i > 0, zp, 0.0)
            z1 = zext_scr[HIST_ROWS - 1:HIST_ROWS - 1 + tm, cs]
            z2 = zext_scr[HIST_ROWS - 2:HIST_ROWS - 2 + tm, cs]
            zt_ref[:, cs] = z[tm - 8:tm]
        else:
            zext_scr[0:HIST_ROWS, cs] = jnp.zeros((HIST_ROWS, tf // 2), F32)
            z1 = jnp.where(r >= 1, zext_scr[HIST_ROWS - 1:HIST_ROWS - 1 + tm, cs], 0.0) + zs1_ref[:, cs]
            z2 = jnp.where(r >= 2, zext_scr[HIST_ROWS - 2:HIST_ROWS - 2 + tm, cs], 0.0) + zs2_ref[:, cs]
            zt_ref[:, cs] = z
        zc = cb_ref[:, cs] + z2 * cw_ref[0:1, cs] + z1 * cw_ref[1:2, cs] + z * cw_ref[2:3, cs]
        acts.append((jax.nn.silu(zc) * val).astype(BF16))
    acc_scr[...] += _dot(jnp.concatenate(acts, axis=1), wo_ref[...])

    @pl.when(f == pl.num_programs(1) - 1)
    def _():
        y_ref[...] = x_ref[...] + acc_scr[...]


def _ffn(x1, g, wz_bf, wv_bf, cw, cb, wo_bf, *, tm, seq_len, zs1=None, zs2=None):
    m, d = x1.shape
    fp = wz_bf.shape[1]
    tf = FF_TILE
    streaming = zs1 is None
    row_tile = lambda i, f: (i, 0)
    col_tile = lambda i, f: (0, f)
    w_specs = [
        pl.BlockSpec((1, d), lambda i, f: (0, 0)),
        pl.BlockSpec((d, tf), col_tile),
        pl.BlockSpec((d, tf), col_tile),
        pl.BlockSpec((CONV_WIDTH, tf), col_tile),
        pl.BlockSpec((1, tf), col_tile),
        pl.BlockSpec((tf, d), lambda i, f: (f, 0)),
    ]
    scratch = [pltpu.VMEM((tm, d), BF16)]
    if streaming:
        assert seq_len == m and tm % HIST_ROWS == 0
        per_tile = tm // HIST_ROWS
        in_specs = [pl.BlockSpec((tm, d), row_tile),
                    pl.BlockSpec((HIST_ROWS, d), lambda i, f: (jnp.maximum(i * per_tile - 1, 0), 0))] + w_specs
        args = (x1, x1, g, wz_bf, wv_bf, cw, cb, wo_bf)
        zt_shape, zt_spec = (m // tm * 8, fp), pl.BlockSpec((8, tf), lambda i, f: (i, f))
        scratch.append(pltpu.VMEM((HIST_ROWS, d), BF16))
    else:
        assert tm % seq_len == 0 and seq_len >= CONV_WIDTH - 1
        in_specs = [pl.BlockSpec((tm, d), row_tile),
                    pl.BlockSpec((tm, tf), lambda i, f: (i, f)),
                    pl.BlockSpec((tm, tf), lambda i, f: (i, f))] + w_specs
        args = (x1, zs1, zs2, g, wz_bf, wv_bf, cw, cb, wo_bf)
        zt_shape, zt_spec = (m, fp), pl.BlockSpec((tm, tf), lambda i, f: (i, f))
    scratch += [pltpu.VMEM((HIST_ROWS + tm, tf), F32), pltpu.VMEM((tm, d), F32)]
    block_bytes = (2 * _nbytes((tm, d), F32) + 2 * _nbytes((d, tf), BF16) + _nbytes((tf, d), BF16)
                   + 3 * _nbytes((tm, tf), F32))
    scratch_bytes = _nbytes((tm, d), BF16) + _nbytes((HIST_ROWS + tm, tf), F32) + _nbytes((tm, d), F32)
    return pl.pallas_call(
        functools.partial(_ffn_kernel, tm=tm, seq_len=seq_len, streaming=streaming),
        grid=(m // tm, fp // tf),
        in_specs=in_specs,
        out_specs=(pl.BlockSpec((tm, d), row_tile), zt_spec),
        out_shape=(jax.ShapeDtypeStruct((m, d), F32), jax.ShapeDtypeStruct(zt_shape, F32)),
        scratch_shapes=scratch,
        compiler_params=pltpu.CompilerParams(
            dimension_semantics=("arbitrary", "arbitrary"),
            vmem_limit_bytes=_vmem_limit(block_bytes, scratch_bytes)),
        name="conv_glu_ffn",
    )(*args)


def _prep_weights(attn_norm_g, w_in, q_norm_g, k_norm_g, lambda_q1, lambda_k1, lambda_q2, lambda_k2,
                  subln_g, w_pool, pool_scale, w_up_attn, w_up_pool, w_out,
                  ffn_norm_g, w_ffn_in, conv_w, conv_b, w_ffn_out, *, n_heads):
    d_ff = conv_w.shape[1]
    fp = -(-d_ff // FF_TILE) * FF_TILE
    pad = fp - d_ff
    lam_vec = jnp.pad(jnp.stack([lambda_q1, lambda_k1, lambda_q2, lambda_k2]).astype(F32),
                      ((0, 0), (0, LANES - lambda_q1.shape[0])))
    group = np.arange(MXU_TILE) // QK_HALF
    gd = jnp.asarray((group[:, None] == group[None, :]).astype(np.float32) / QK_HALF, BF16)
    score_bound = (QK_HALF ** 0.5 * BOUND_MARGIN * jnp.max(jnp.abs(q_norm_g), axis=1)
                   * jnp.max(jnp.abs(k_norm_g), axis=1)).astype(F32)
    use_bound = (jnp.max(score_bound) <= MAX_FIXED_SHIFT).astype(jnp.int32).reshape(1)
    wz, wv = _split_ffn_in(w_ffn_in, d_ff=d_ff, fp=fp, rows=CAST_ROWS_WIDE)
    return dict(
        score_bound=score_bound,
        use_bound=use_bound,
        attn_g=attn_norm_g.reshape(1, -1),
        w_in=_to_bf16(w_in, rows=CAST_ROWS_WIDE),
        qg=(jnp.tile(q_norm_g.reshape(-1), n_heads) * (QK_HALF ** -0.5)).reshape(1, -1),
        kg=jnp.tile(k_norm_g.reshape(-1), n_heads).reshape(1, -1),
        gd=gd,
        lam_vec=lam_vec,
        sg=subln_g.reshape(1, -1),
        w_pool=w_pool.astype(BF16),
        pool_scale=pool_scale.reshape(1, -1),
        w_up_attn=_to_bf16(w_up_attn, rows=CAST_ROWS),
        w_up_pool=_to_bf16(w_up_pool, rows=CAST_ROWS),
        w_out=_to_bf16(w_out, rows=CAST_ROWS),
        ffn_g=ffn_norm_g.reshape(1, -1),
        wz=wz,
        wv=wv,
        conv_w=jnp.pad(conv_w, ((0, 0), (0, pad))),
        conv_b=jnp.pad(conv_b, (0, pad)).reshape(1, -1),
        w_ffn_out=_pad_rows_bf16(w_ffn_out, fp=fp, rows=LANES),
        d_ff=d_ff,
    )


def _layer(x, start, caches, layer, pool_hist, conv_hist, lam_init, w, *, n_heads, tm, tq):
    n_seq, t_len, d = x.shape
    m = n_seq * t_len
    d_ff = w["d_ff"]
    x2 = x.reshape(m, d)

    q, k_f32, kb, v_f32, vb, u, sig = _in_proj(
        x2, w["attn_g"], w["w_in"], w["qg"], w["kg"], w["gd"], tm=min(tm, INPROJ_ROWS), n_heads=n_heads)

    if caches is None:
        o = _attn_prompt(q, kb, vb, w["lam_vec"], w["sg"], w["score_bound"], w["use_bound"],
                         tq=tq, tk=tq, lam_init=lam_init)
    else:
        o = _attn_sample(q, kb, vb, caches[0], caches[1], w["lam_vec"], w["sg"],
                         layer=layer, t_new=t_len, lam_init=lam_init)

    hist = jnp.pad(pool_hist, ((0, 0), (HIST_ROWS - POOL_HIST, 0), (0, 0))).reshape(n_seq * HIST_ROWS, -1)
    pool_tm = min(tm, t_len)
    pb = _pool(u, hist, w["w_pool"], w["pool_scale"], n_seq=n_seq, seq_len=t_len, tm=pool_tm, start=start)

    x1 = _merge(o, pb, sig, x2, w["w_up_attn"], w["w_up_pool"], w["w_out"], tm=tm)

    if conv_hist is None:
        y, z_tail = _ffn(x1, w["ffn_g"], w["wz"], w["wv"], w["conv_w"], w["conv_b"], w["w_ffn_out"],
                         tm=tm, seq_len=t_len)
        conv_state = z_tail[-(CONV_WIDTH - 1):, :d_ff][None]
    else:
        fpad = w["conv_b"].shape[1] - d_ff
        zs1 = jnp.pad(conv_hist[:, 1:2], ((0, 0), (0, t_len - 1), (0, fpad))).reshape(m, -1)
        zs2 = jnp.pad(conv_hist, ((0, 0), (0, t_len - (CONV_WIDTH - 1)), (0, fpad))).reshape(m, -1)
        y, z_all = _ffn(x1, w["ffn_g"], w["wz"], w["wv"], w["conv_w"], w["conv_b"], w["w_ffn_out"],
                        tm=tm, seq_len=t_len, zs1=zs1, zs2=zs2)
        conv_state = z_all.reshape(n_seq, t_len, -1)[:, -(CONV_WIDTH - 1):, :d_ff]

    u3 = u.reshape(n_seq, t_len, -1)
    pool_state = jnp.concatenate([pool_hist, u3], axis=1)[:, -POOL_HIST:] if t_len < POOL_HIST else u3[:, -POOL_HIST:]
    return (y.reshape(n_seq, t_len, d),
            k_f32.reshape(n_seq, t_len, n_heads, HEAD_W),
            v_f32.reshape(n_seq, t_len, n_heads, HEAD_W),
            pool_state, conv_state)


def kernel(x_prompt, x_sample, cache_k, cache_v, state_pool, state_conv, attn_norm_g, w_in, q_norm_g, k_norm_g, lambda_q1, lambda_k1, lambda_q2, lambda_k2, subln_g, w_pool, pool_scale, w_up_attn, w_up_pool, w_out, ffn_norm_g, w_ffn_in, conv_w, conv_b, w_ffn_out):
    depth = w_in.shape[0]
    n_heads = cache_k.shape[3]
    past = cache_k.shape[2]
    y_p, y_s = x_prompt, x_sample
    outs = [[] for _ in range(8)]
    for l in range(depth):
        lam_init = 0.8 - 0.6 * math.exp(-0.3 * l)
        w = _prep_weights(attn_norm_g[l], w_in[l], q_norm_g[l], k_norm_g[l], lambda_q1[l], lambda_k1[l],
                          lambda_q2[l], lambda_k2[l], subln_g[l], w_pool[l], pool_scale[l], w_up_attn[l],
                          w_up_pool[l], w_out[l], ffn_norm_g[l], w_ffn_in[l], conv_w[l], conv_b[l],
                          w_ffn_out[l], n_heads=n_heads)
        pool0 = jnp.zeros((x_prompt.shape[0], POOL_HIST, state_pool.shape[-1]), F32)
        y_p, kp, vp, pp, cp = _layer(y_p, 0, None, l, pool0, None, lam_init, w,
                                     n_heads=n_heads, tm=512, tq=512)
        y_s, ks, vs, ps, cs = _layer(y_s, past, (cache_k, cache_v), l, state_pool[l], state_conv[l],
                                     lam_init, w, n_heads=n_heads,
                                     tm=x_sample.shape[0] * x_sample.shape[1], tq=None)
        for lst, val in zip(outs, (kp, vp, pp, cp, ks, vs, ps, cs)):
            lst.append(val)
    return (y_p, y_s) + tuple(jnp.stack(lst) for lst in outs)
```

```python
import functools
import math

import numpy as np
import jax
import jax.numpy as jnp
from jax import lax
from jax.experimental import pallas as pl
from jax.experimental.pallas import tpu as pltpu

F32 = jnp.float32
BF16 = jnp.bfloat16

EPS = 1e-6
NEG_INF = -1e30
CHUNK = 64
POOL_WINDOWS = (2, 4, 8, 16)
POOL_HIST = max(POOL_WINDOWS) - 1
CONV_WIDTH = 3

LANES = 128
HIST_ROWS = 16
MXU_TILE = 256
VMEM_LIMIT_CAP = 60 * 1024 * 1024
VMEM_TEMP_ALLOWANCE = 12 * 1024 * 1024

HEAD_W = 128
QK_HALF = 64
FF_TILE = 512
SAMPLE_POS_CHUNK = 512
INPROJ_ROWS = 256
FFN_ROWS = 256
CAST_ROWS = 512
CAST_ROWS_WIDE = 128
MAX_FIXED_SHIFT = 40.0
BOUND_MARGIN = 1.02


def _vmem_limit(block_bytes, scratch_bytes):
    est = 2 * block_bytes + scratch_bytes + VMEM_TEMP_ALLOWANCE
    return int(min(est, VMEM_LIMIT_CAP))


def _nbytes(shape, dtype):
    return int(np.prod(shape)) * jnp.dtype(dtype).itemsize


def _rms_rows(x, g):
    ms = jnp.mean(x * x, axis=-1, keepdims=True)
    return x * lax.rsqrt(ms + EPS) * g


def _dot(a, b):
    return jnp.dot(a, b, preferred_element_type=F32)


def _dot_nt(a, b):
    return lax.dot_general(a, b, (((1,), (1,)), ((), ())), preferred_element_type=F32)


def _cast_kernel(x_ref, o_ref):
    o_ref[...] = x_ref[...].astype(BF16)


def _to_bf16(w, *, rows):
    r, c = w.shape
    return pl.pallas_call(
        _cast_kernel,
        grid=(r // rows,),
        in_specs=[pl.BlockSpec((rows, c), lambda i: (i, 0))],
        out_specs=pl.BlockSpec((rows, c), lambda i: (i, 0)),
        out_shape=jax.ShapeDtypeStruct((r, c), BF16),
        compiler_params=pltpu.CompilerParams(
            dimension_semantics=("arbitrary",),
            vmem_limit_bytes=_vmem_limit(_nbytes((rows, c), F32) + _nbytes((rows, c), BF16), 0)),
        name="cast_bf16",
    )(w)


def _interleave_cast_kernel(x_ref, o_ref, *, d_ff):
    rows = x_ref.shape[0]
    for n in range(o_ref.shape[1] // (2 * MXU_TILE)):
        lo = n * MXU_TILE
        width = max(0, min(MXU_TILE, d_ff - lo))
        for part, src in ((0, lo), (1, d_ff + lo)):
            dst = (2 * n + part) * MXU_TILE
            if width:
                o_ref[:, dst:dst + width] = x_ref[:, src:src + width].astype(BF16)
            if width < MXU_TILE:
                o_ref[:, dst + width:dst + MXU_TILE] = jnp.zeros((rows, MXU_TILE - width), BF16)


def _interleave_ffn_in(w, *, d_ff, fp, rows):
    d = w.shape[0]
    return pl.pallas_call(
        functools.partial(_interleave_cast_kernel, d_ff=d_ff),
        grid=(d // rows,),
        in_specs=[pl.BlockSpec((rows, 2 * d_ff), lambda i: (i, 0))],
        out_specs=pl.BlockSpec((rows, 2 * fp), lambda i: (i, 0)),
        out_shape=jax.ShapeDtypeStruct((d, 2 * fp), BF16),
        compiler_params=pltpu.CompilerParams(
            dimension_semantics=("arbitrary",),
            vmem_limit_bytes=_vmem_limit(_nbytes((rows, 2 * d_ff), F32) + _nbytes((rows, 2 * fp), BF16), 0)),
        name="interleave_cast_ffn_in",
    )(w)


def _pad_cast_kernel(x_ref, o_ref, *, n_in):
    keep = pl.program_id(0) < n_in
    o_ref[...] = jnp.where(keep, x_ref[...], 0.0).astype(BF16)


def _pad_rows_bf16(w, *, fp, rows):
    d_ff, d = w.shape
    n_in = d_ff // rows
    return pl.pallas_call(
        functools.partial(_pad_cast_kernel, n_in=n_in),
        grid=(fp // rows,),
        in_specs=[pl.BlockSpec((rows, d), lambda i: (jnp.minimum(i, n_in - 1), 0))],
        out_specs=pl.BlockSpec((rows, d), lambda i: (i, 0)),
        out_shape=jax.ShapeDtypeStruct((fp, d), BF16),
        compiler_params=pltpu.CompilerParams(
            dimension_semantics=("arbitrary",),
            vmem_limit_bytes=_vmem_limit(_nbytes((rows, d), F32) + _nbytes((rows, d), BF16), 0)),
        name="pad_cast_ffn_out",
    )(w)


def _inproj_kernel(x_ref, g_ref, w_ref, qg_ref, kg_ref, gd_ref,
                   q_ref, kf_ref, kb_ref, vf_ref, vb_ref, u_ref, sig_ref, h_scr,
                   *, n_heads):
    tm = x_ref.shape[0]
    aw = n_heads * HEAD_W
    n_col = w_ref.shape[1] // aw
    h_scr[...] = _rms_rows(x_ref[...], g_ref[...]).astype(BF16)

    def col_tile(j):
        return _dot(h_scr[...], w_ref[:, j * aw:(j + 1) * aw])

    def qk_norm(y, gain_ref):
        heads = []
        for c in range(aw // MXU_TILE):
            cs = slice(c * MXU_TILE, (c + 1) * MXU_TILE)
            yc = y[:, cs]
            ms = _dot((yc * yc).astype(BF16), gd_ref[...])
            yn = yc * lax.rsqrt(ms + EPS) * gain_ref[:, cs]
            heads += [yn[:, :HEAD_W], yn[:, HEAD_W:]]
        return heads

    for j in range(4, n_col):
        sig_ref[:, (j - 4) * aw:(j - 3) * aw] = jax.nn.sigmoid(col_tile(j)).astype(BF16)

    feat = lax.broadcasted_iota(jnp.int32, (tm, HEAD_W), 1)
    for h, yh in enumerate(qk_norm(col_tile(0), qg_ref)):
        q_ref[h, 0] = jnp.where(feat < QK_HALF, yh, 0.0).astype(BF16)
        q_ref[h, 1] = jnp.where(feat >= QK_HALF, yh, 0.0).astype(BF16)

    for h, yh in enumerate(qk_norm(col_tile(1), kg_ref)):
        kf_ref[:, h * HEAD_W:(h + 1) * HEAD_W] = yh
        kb_ref[h] = yh.astype(BF16)

    y = col_tile(2)
    vf_ref[...] = y
    for h in range(n_heads):
        vb_ref[h] = y[:, h * HEAD_W:(h + 1) * HEAD_W].astype(BF16)

    u_ref[...] = col_tile(3)


def _in_proj(x, g, w_bf, qg, kg, gd, *, tm, n_heads):
    m, d = x.shape
    aw = n_heads * HEAD_W
    n_all = w_bf.shape[1]
    gate_w = n_all - 4 * aw
    q_shape, q_spec = (n_heads, 2, m, HEAD_W), pl.BlockSpec((n_heads, 2, tm, HEAD_W), lambda i: (0, 0, i, 0))
    v_shape, v_spec = (n_heads, m, HEAD_W), pl.BlockSpec((n_heads, tm, HEAD_W), lambda i: (0, i, 0))
    out_shape = (
        jax.ShapeDtypeStruct(q_shape, BF16),
        jax.ShapeDtypeStruct((m, aw), F32),
        jax.ShapeDtypeStruct((n_heads, m, HEAD_W), BF16),
        jax.ShapeDtypeStruct((m, aw), F32),
        jax.ShapeDtypeStruct(v_shape, BF16),
        jax.ShapeDtypeStruct((m, aw), F32),
        jax.ShapeDtypeStruct((m, gate_w), BF16),
    )
    resident = pl.Buffered(1)
    in_specs = [
        pl.BlockSpec((tm, d), lambda i: (i, 0)),
        pl.BlockSpec((1, d), lambda i: (0, 0)),
        pl.BlockSpec((d, n_all), lambda i: (0, 0), pipeline_mode=resident),
        pl.BlockSpec((1, aw), lambda i: (0, 0)),
        pl.BlockSpec((1, aw), lambda i: (0, 0)),
        pl.BlockSpec((MXU_TILE, MXU_TILE), lambda i: (0, 0)),
    ]
    out_specs = (
        q_spec,
        pl.BlockSpec((tm, aw), lambda i: (i, 0)),
        pl.BlockSpec((n_heads, tm, HEAD_W), lambda i: (0, i, 0)),
        pl.BlockSpec((tm, aw), lambda i: (i, 0)),
        v_spec,
        pl.BlockSpec((tm, aw), lambda i: (i, 0)),
        pl.BlockSpec((tm, gate_w), lambda i: (i, 0)),
    )
    block_bytes = (_nbytes((tm, d), F32) + 3 * _nbytes((tm, aw), F32) + 4 * _nbytes((tm, aw), BF16)
                   + _nbytes((tm, gate_w), BF16))
    return pl.pallas_call(
        functools.partial(_inproj_kernel, n_heads=n_heads),
        grid=(m // tm,),
        in_specs=in_specs,
        out_specs=out_specs,
        out_shape=out_shape,
        scratch_shapes=[pltpu.VMEM((tm, d), BF16)],
        compiler_params=pltpu.CompilerParams(
            dimension_semantics=("arbitrary",),
            vmem_limit_bytes=_vmem_limit(block_bytes, _nbytes((d, n_all), BF16) + _nbytes((tm, d), BF16))),
        name="in_proj",
    )(x, g, w_bf, qg, kg, gd)


def _lambda_value(lam_ref, lam_init):
    a = lam_ref[...]
    s1 = jnp.sum(a[0:1] * a[1:2], axis=1, keepdims=True)
    s2 = jnp.sum(a[2:3] * a[3:4], axis=1, keepdims=True)
    return jnp.exp(s1) - jnp.exp(s2) + lam_init


def _sub_ln(o, sg, lam_init):
    o = o * lax.rsqrt(jnp.mean(o * o, axis=-1, keepdims=True) + EPS)
    return o * sg * (1.0 - lam_init)


def _attn_kernel(qi_ref, kj_ref, flag_ref, fast_ref, bound_ref, lam_ref, sg_ref, q_ref, k_ref, v_ref, o_ref,
                 m_scr, l_scr, acc_scr, *, tq, tk, n_heads, lam_init):
    t = pl.program_id(0)
    i = qi_ref[t]
    j = kj_ref[t]
    flags = flag_ref[t]
    is_first = (flags & 1) != 0
    is_last = (flags & 2) != 0
    is_masked = (flags & 4) != 0
    fast = fast_ref[0] != 0

    @pl.when(is_first)
    def _():
        m_scr[...] = jnp.full(m_scr.shape, NEG_INF, F32)
        l_scr[...] = jnp.zeros(l_scr.shape, F32)
        acc_scr[...] = jnp.zeros(acc_scr.shape, F32)

    def step(masked, bounded):
        if masked:
            qpos = i * tq + lax.broadcasted_iota(jnp.int32, (tq, tk), 0)
            kpos = j * tk + lax.broadcasted_iota(jnp.int32, (tq, tk), 1)
            mask = (kpos // CHUNK) <= (qpos // CHUNK)

        def head(h, carry):
            kh = k_ref[h]
            vh = v_ref[h]
            for c in range(2):
                s = _dot_nt(q_ref[h, c], kh)
                if bounded:
                    s = s - bound_ref[c]
                if masked:
                    s = jnp.where(mask, s, NEG_INF)
                if bounded:
                    p = jnp.exp(s)
                    psum = p[:, :LANES]
                    for n in range(1, tk // LANES):
                        psum = psum + p[:, n * LANES:(n + 1) * LANES]
                    l_scr[h, c] += psum
                    acc_scr[h, c] += _dot(p.astype(BF16), vh)
                else:
                    m_prev = m_scr[h, c]
                    m_next = jnp.maximum(m_prev, jnp.max(s, axis=1, keepdims=True))
                    p = jnp.exp(s - jnp.tile(m_next, (1, tk // LANES)))
                    alpha = jnp.exp(m_prev - m_next)
                    l_scr[h, c] = alpha * l_scr[h, c] + jnp.sum(p, axis=1, keepdims=True)
                    m_scr[h, c] = m_next
                    acc_scr[h, c] = alpha * acc_scr[h, c] + _dot(p.astype(BF16), vh)
            return carry

        lax.fori_loop(0, n_heads, head, 0, unroll=bounded)

    for masked in (False, True):
        for bounded in (False, True):
            pl.when((is_masked == masked) & (fast == bounded))(functools.partial(step, masked, bounded))

    def finalize(bounded):
        lam = _lambda_value(lam_ref, lam_init)

        def row_sum(h, c):
            l = l_scr[h, c]
            return jnp.sum(l, axis=1, keepdims=True) if bounded else l

        def head(h, carry):
            o = acc_scr[h, 0] / row_sum(h, 0) - lam * (acc_scr[h, 1] / row_sum(h, 1))
            o_ref[h] = _sub_ln(o, sg_ref[...], lam_init).astype(BF16)
            return carry

        lax.fori_loop(0, n_heads, head, 0)

    for bounded in (False, True):
        pl.when(is_last & (fast == bounded))(functools.partial(finalize, bounded))


def _attn_schedule(seq, tq, tk):
    qi, kj, flags = [], [], []
    for i in range(seq // tq):
        q_lo, q_hi = i * tq, (i + 1) * tq - 1
        js = [j for j in range(seq // tk) if (j * tk) // CHUNK <= q_hi // CHUNK]
        for n, j in enumerate(js):
            masked = ((j + 1) * tk - 1) // CHUNK > q_lo // CHUNK
            qi.append(i)
            kj.append(j)
            flags.append((1 if n == 0 else 0) | (2 if n == len(js) - 1 else 0) | (4 if masked else 0))
    return (np.asarray(qi, np.int32), np.asarray(kj, np.int32), np.asarray(flags, np.int32))


def _attn_prompt(q, kb, vb, lam_vec, sg, score_bound, use_bound, *, tq, tk, lam_init):
    n_heads, _, seq, _ = q.shape
    qi, kj, flags = _attn_schedule(seq, tq, tk)
    stat = (n_heads, 2, tq, LANES)
    grid_spec = pltpu.PrefetchScalarGridSpec(
        num_scalar_prefetch=5,
        grid=(len(qi),),
        in_specs=[
            pl.BlockSpec((4, LANES), lambda t, qi, kj, *_: (0, 0)),
            pl.BlockSpec((1, HEAD_W), lambda t, qi, kj, *_: (0, 0)),
            pl.BlockSpec((n_heads, 2, tq, HEAD_W), lambda t, qi, kj, *_: (0, 0, qi[t], 0)),
            pl.BlockSpec((n_heads, tk, HEAD_W), lambda t, qi, kj, *_: (0, kj[t], 0)),
            pl.BlockSpec((n_heads, tk, HEAD_W), lambda t, qi, kj, *_: (0, kj[t], 0)),
        ],
        out_specs=pl.BlockSpec((n_heads, tq, HEAD_W), lambda t, qi, kj, *_: (0, qi[t], 0)),
        scratch_shapes=[pltpu.VMEM(stat, F32), pltpu.VMEM(stat, F32), pltpu.VMEM(stat, F32)],
    )
    block_bytes = (_nbytes((n_heads, 2, tq, HEAD_W), BF16) + 2 * _nbytes((n_heads, tk, HEAD_W), BF16)
                   + _nbytes((n_heads, tq, HEAD_W), BF16))
    return pl.pallas_call(
        functools.partial(_attn_kernel, tq=tq, tk=tk, n_heads=n_heads, lam_init=lam_init),
        grid_spec=grid_spec,
        out_shape=jax.ShapeDtypeStruct((n_heads, seq, HEAD_W), BF16),
        compiler_params=pltpu.CompilerParams(
            dimension_semantics=("arbitrary",),
            vmem_limit_bytes=_vmem_limit(block_bytes, 3 * _nbytes(stat, F32))),
        name="attn_prompt",
    )(jnp.asarray(qi), jnp.asarray(kj), jnp.asarray(flags), use_bound, score_bound, lam_vec, sg, q, kb, vb)


def _attn_dec_kernel(lam_ref, sg_ref, bias_c_ref, bias_n_ref, q_ref, kn_ref, vn_ref, ck_ref, cv_ref, o_ref,
                     *, pos_chunk, t_new, n_heads, lam_init):
    lam = _lambda_value(lam_ref, lam_init)
    past = ck_ref.shape[0]
    rows = n_heads * 2 * t_new
    q = q_ref[...].reshape(rows, HEAD_W)

    def fold(carry, k_flat, v_flat, bias):
        m, l, acc = carry
        s = _dot_nt(q, k_flat) + bias
        m_new = jnp.maximum(m, jnp.max(s, axis=1, keepdims=True))
        alpha = jnp.exp(m - m_new)
        p = jnp.exp(s - m_new)
        return (m_new, alpha * l + jnp.sum(p, axis=1, keepdims=True), alpha * acc + _dot(p.astype(BF16), v_flat))

    carry = (jnp.full((rows, 1), NEG_INF, F32), jnp.zeros((rows, 1), F32), jnp.zeros((rows, HEAD_W), F32))
    for c0 in range(0, past, pos_chunk):
        k_flat = ck_ref[c0:c0 + pos_chunk].reshape(pos_chunk * n_heads, HEAD_W).astype(BF16)
        v_flat = cv_ref[c0:c0 + pos_chunk].reshape(pos_chunk * n_heads, HEAD_W).astype(BF16)
        carry = fold(carry, k_flat, v_flat, bias_c_ref[...])
    carry = fold(carry, kn_ref[...].reshape(n_heads * t_new, HEAD_W), vn_ref[...].reshape(n_heads * t_new, HEAD_W),
                 bias_n_ref[...])
    _, l, acc = carry
    o_all = acc / l
    for h in range(n_heads):
        base = h * 2 * t_new
        o = o_all[base:base + t_new] - lam * o_all[base + t_new:base + 2 * t_new]
        o_ref[h] = _sub_ln(o, sg_ref[...], lam_init).astype(BF16)


def _sample_biases(n_heads, t_new, past, pos_chunk):
    rows = n_heads * 2 * t_new
    row_head = (np.arange(rows) // (2 * t_new))[:, None]
    q_chunk = ((past + np.arange(rows) % t_new) // CHUNK)[:, None]
    assert (past - 1) // CHUNK <= past // CHUNK
    col = np.arange(pos_chunk * n_heads)[None, :]
    bias_c = np.where(col % n_heads == row_head, 0.0, NEG_INF).astype(np.float32)
    col = np.arange(n_heads * t_new)[None, :]
    visible = (col // t_new == row_head) & ((past + col % t_new) // CHUNK <= q_chunk)
    bias_n = np.where(visible, 0.0, NEG_INF).astype(np.float32)
    return jnp.asarray(bias_c), jnp.asarray(bias_n)


def _attn_sample(q, kb, vb, cache_k, cache_v, lam_vec, sg, *, layer, t_new, lam_init):
    n_heads = q.shape[0]
    _, n_seq, past, _, _ = cache_k.shape
    aw = n_heads * HEAD_W
    pos_chunk = min(past, SAMPLE_POS_CHUNK)
    assert past % pos_chunk == 0
    bias_c, bias_n = _sample_biases(n_heads, t_new, past, pos_chunk)
    cache_spec = pl.BlockSpec((None, None, past, n_heads, HEAD_W), lambda b: (layer, b, 0, 0, 0))
    block_bytes = 2 * _nbytes((past, aw), F32) + 5 * _nbytes((n_heads, t_new, HEAD_W), BF16)
    return pl.pallas_call(
        functools.partial(_attn_dec_kernel, pos_chunk=pos_chunk, t_new=t_new, n_heads=n_heads,
                          lam_init=lam_init),
        grid=(n_seq,),
        in_specs=[
            pl.BlockSpec((4, LANES), lambda b: (0, 0)),
            pl.BlockSpec((1, HEAD_W), lambda b: (0, 0)),
            pl.BlockSpec(bias_c.shape, lambda b: (0, 0), pipeline_mode=pl.Buffered(1)),
            pl.BlockSpec(bias_n.shape, lambda b: (0, 0), pipeline_mode=pl.Buffered(1)),
            pl.BlockSpec((n_heads, 2, t_new, HEAD_W), lambda b: (0, 0, b, 0)),
            pl.BlockSpec((n_heads, t_new, HEAD_W), lambda b: (0, b, 0)),
            pl.BlockSpec((n_heads, t_new, HEAD_W), lambda b: (0, b, 0)),
            cache_spec,
            cache_spec,
        ],
        out_specs=pl.BlockSpec((n_heads, t_new, HEAD_W), lambda b: (0, b, 0)),
        out_shape=jax.ShapeDtypeStruct((n_heads, n_seq * t_new, HEAD_W), BF16),
        compiler_params=pltpu.CompilerParams(
            dimension_semantics=("arbitrary",),
            vmem_limit_bytes=_vmem_limit(block_bytes, _nbytes(bias_c.shape, F32) + _nbytes(bias_n.shape, F32))),
        name="attn_sample",
    )(lam_vec, sg, bias_c, bias_n, q, kb, vb, cache_k, cache_v)


def _pool_kernel(u_ref, hist_ref, wp_ref, ps_ref, pb_ref, ext_scr, *, tm, start):
    ti = pl.program_id(1)

    @pl.when(ti == 0)
    def _():
        ext_scr[0:HIST_ROWS] = hist_ref[...]

    @pl.when(ti > 0)
    def _():
        ext_scr[0:HIST_ROWS] = ext_scr[tm:tm + HIST_ROWS]

    ext_scr[HIST_ROWS:HIST_ROWS + tm] = u_ref[...]
    pos = start + ti * tm + lax.broadcasted_iota(jnp.int32, (tm, 1), 0)
    gw = wp_ref.shape[1]
    for g, w in enumerate(POOL_WINDOWS):
        cs = slice(g * gw, (g + 1) * gw)
        cur = ext_scr[HIST_ROWS:HIST_ROWS + tm, cs]
        win = cur
        for back in range(1, w):
            win = win + ext_scr[HIST_ROWS - back:HIST_ROWS - back + tm, cs]
        cnt = jnp.minimum(w, pos + 1).astype(F32)
        p = win / cnt - cur
        y = _dot(p.astype(BF16), wp_ref[g])
        pb_ref[:, cs] = (y * ps_ref[:, cs]).astype(BF16)


def _pool(u, hist, wp_bf, ps, *, n_seq, seq_len, tm, start):
    m, pw = u.shape
    tiles = seq_len // tm
    n_groups, gw, _ = wp_bf.shape
    block_bytes = (_nbytes((tm, pw), F32) + _nbytes((HIST_ROWS, pw), F32) + _nbytes(wp_bf.shape, BF16)
                   + _nbytes((tm, pw), BF16))
    return pl.pallas_call(
        functools.partial(_pool_kernel, tm=tm, start=start),
        grid=(n_seq, tiles),
        in_specs=[
            pl.BlockSpec((tm, pw), lambda b, t: (b * tiles + t, 0)),
            pl.BlockSpec((HIST_ROWS, pw), lambda b, t: (b, 0)),
            pl.BlockSpec((n_groups, gw, gw), lambda b, t: (0, 0, 0)),
            pl.BlockSpec((1, pw), lambda b, t: (0, 0)),
        ],
        out_specs=pl.BlockSpec((tm, pw), lambda b, t: (b * tiles + t, 0)),
        out_shape=jax.ShapeDtypeStruct((m, pw), BF16),
        scratch_shapes=[pltpu.VMEM((HIST_ROWS + tm, pw), F32)],
        compiler_params=pltpu.CompilerParams(
            dimension_semantics=("arbitrary", "arbitrary"),
            vmem_limit_bytes=_vmem_limit(block_bytes, _nbytes((HIST_ROWS + tm, pw), F32))),
        name="pool_mix",
    )(u, hist, wp_bf, ps)


def _merge_kernel(o_ref, pb_ref, sig_ref, x_ref, wua_ref, wup_ref, wout_ref, g2_ref, x1_ref, h2_ref, m_scr,
                  *, n_heads):
    d = x_ref.shape[1]
    o = jnp.concatenate([o_ref[h] for h in range(n_heads)], axis=1)
    pb = pb_ref[...]
    for n in range(d // FF_TILE):
        cs = slice(n * FF_TILE, (n + 1) * FF_TILE)
        gs = slice(d + n * FF_TILE, d + (n + 1) * FF_TILE)
        a = _dot(o, wua_ref[:, cs])
        b = _dot(pb, wup_ref[:, cs])
        m_scr[:, cs] = (sig_ref[:, cs].astype(F32) * a + sig_ref[:, gs].astype(F32) * b).astype(BF16)
    for n in range(d // FF_TILE):
        cs = slice(n * FF_TILE, (n + 1) * FF_TILE)
        x1_ref[:, cs] = x_ref[:, cs] + _dot(m_scr[...], wout_ref[:, cs])
    h2_ref[...] = _rms_rows(x1_ref[...], g2_ref[...]).astype(BF16)


def _merge(o, pb, sig, x, wua_bf, wup_bf, wout_bf, ffn_g, *, tm):
    n_heads = o.shape[0]
    m, d = x.shape
    aw = n_heads * HEAD_W
    block_bytes = (2 * _nbytes((tm, aw), BF16) + _nbytes((tm, 2 * d), BF16) + 2 * _nbytes((tm, d), F32)
                   + _nbytes((tm, d), BF16))
    weight_bytes = 2 * _nbytes((aw, d), BF16) + _nbytes((d, d), BF16)
    resident = pl.Buffered(1)
    return pl.pallas_call(
        functools.partial(_merge_kernel, n_heads=n_heads),
        grid=(m // tm,),
        in_specs=[
            pl.BlockSpec((n_heads, tm, HEAD_W), lambda i: (0, i, 0)),
            pl.BlockSpec((tm, aw), lambda i: (i, 0)),
            pl.BlockSpec((tm, 2 * d), lambda i: (i, 0)),
            pl.BlockSpec((tm, d), lambda i: (i, 0)),
            pl.BlockSpec((aw, d), lambda i: (0, 0), pipeline_mode=resident),
            pl.BlockSpec((aw, d), lambda i: (0, 0), pipeline_mode=resident),
            pl.BlockSpec((d, d), lambda i: (0, 0), pipeline_mode=resident),
            pl.BlockSpec((1, d), lambda i: (0, 0)),
        ],
        out_specs=(pl.BlockSpec((tm, d), lambda i: (i, 0)), pl.BlockSpec((tm, d), lambda i: (i, 0))),
        out_shape=(jax.ShapeDtypeStruct((m, d), F32), jax.ShapeDtypeStruct((m, d), BF16)),
        scratch_shapes=[pltpu.VMEM((tm, d), BF16)],
        compiler_params=pltpu.CompilerParams(
            dimension_semantics=("arbitrary",),
            vmem_limit_bytes=_vmem_limit(block_bytes, weight_bytes + _nbytes((tm, d), BF16))),
        name="merge_out_proj",
    )(o, pb, sig, x, wua_bf, wup_bf, wout_bf, ffn_g)


def _ffn_up_kernel(*refs, tm, seq_len, streaming):
    if streaming:
        h_ref, hp_ref, w_ref, cw_ref, cb_ref, act_ref, zt_ref, zext_scr, hext_scr = refs
        hext_scr[0:HIST_ROWS] = hp_ref[...]
        hext_scr[HIST_ROWS:HIST_ROWS + tm] = h_ref[...]
        lhs_ref = hext_scr
        not_first = pl.program_id(0) > 0
    else:
        h_ref, zs1_ref, zs2_ref, w_ref, cw_ref, cb_ref, act_ref, zt_ref, zext_scr = refs
        lhs_ref = h_ref
        r = lax.broadcasted_iota(jnp.int32, (tm, 1), 0) % seq_len
    for n in range(w_ref.shape[1] // (2 * MXU_TILE)):
        cs = slice(n * MXU_TILE, (n + 1) * MXU_TILE)
        zv = _dot(lhs_ref[...], w_ref[:, 2 * n * MXU_TILE:(2 * n + 2) * MXU_TILE])
        if streaming:
            front = jnp.where(not_first, zv[:HIST_ROWS, :MXU_TILE], 0.0)
            z = zv[HIST_ROWS:, :MXU_TILE]
            val = zv[HIST_ROWS:, MXU_TILE:]
            zext_scr[0:HIST_ROWS, cs] = front
            zext_scr[HIST_ROWS:HIST_ROWS + tm, cs] = z
            z1 = zext_scr[HIST_ROWS - 1:HIST_ROWS - 1 + tm, cs]
            z2 = zext_scr[HIST_ROWS - 2:HIST_ROWS - 2 + tm, cs]
            zt_ref[:, cs] = z[tm - 8:tm]
        else:
            z = zv[:, :MXU_TILE]
            val = zv[:, MXU_TILE:]
            zext_scr[0:HIST_ROWS, cs] = jnp.zeros((HIST_ROWS, MXU_TILE), F32)
            zext_scr[HIST_ROWS:HIST_ROWS + tm, cs] = z
            z1 = jnp.where(r >= 1, zext_scr[HIST_ROWS - 1:HIST_ROWS - 1 + tm, cs], 0.0) + zs1_ref[:, cs]
            z2 = jnp.where(r >= 2, zext_scr[HIST_ROWS - 2:HIST_ROWS - 2 + tm, cs], 0.0) + zs2_ref[:, cs]
            zt_ref[:, cs] = z
        zc = cb_ref[:, cs] + z2 * cw_ref[0:1, cs] + z1 * cw_ref[1:2, cs] + z * cw_ref[2:3, cs]
        act_ref[:, cs] = (jax.nn.silu(zc) * val).astype(BF16)


def _ffn_down_kernel(x_ref, a0_ref, a1_ref, w0_ref, w1_ref, y_ref):
    y_ref[...] = x_ref[...] + _dot(a0_ref[...], w0_ref[...]) + _dot(a1_ref[...], w1_ref[...])


def _ffn_up(h2, wcat_bf, cw, cb, *, half, tm, seq_len, zs1=None, zs2=None):
    m, d = h2.shape
    fh = wcat_bf.shape[1] // 4
    streaming = zs1 is None
    w_specs = [
        pl.BlockSpec((d, 2 * fh), lambda i: (0, half), pipeline_mode=pl.Buffered(1)),
        pl.BlockSpec((CONV_WIDTH, fh), lambda i: (0, half)),
        pl.BlockSpec((1, fh), lambda i: (0, half)),
    ]
    scratch = [pltpu.VMEM((HIST_ROWS + tm, fh), F32)]
    scratch_bytes = _nbytes((d, 2 * fh), BF16) + _nbytes((HIST_ROWS + tm, fh), F32)
    if streaming:
        assert seq_len == m and tm % HIST_ROWS == 0
        per_tile = tm // HIST_ROWS
        in_specs = [pl.BlockSpec((tm, d), lambda i: (i, 0)),
                    pl.BlockSpec((HIST_ROWS, d), lambda i: (jnp.maximum(i * per_tile - 1, 0), 0))] + w_specs
        args = (h2, h2, wcat_bf, cw, cb)
        zt_shape, zt_spec = (m // tm * 8, fh), pl.BlockSpec((8, fh), lambda i: (i, 0))
        block_bytes = _nbytes((tm, d), BF16) + _nbytes((tm, fh), BF16)
        scratch.append(pltpu.VMEM((HIST_ROWS + tm, d), BF16))
        scratch_bytes += _nbytes((HIST_ROWS + tm, d), BF16)
    else:
        assert tm % seq_len == 0 and seq_len >= CONV_WIDTH - 1
        in_specs = [pl.BlockSpec((tm, d), lambda i: (i, 0)),
                    pl.BlockSpec((tm, fh), lambda i: (i, half)),
                    pl.BlockSpec((tm, fh), lambda i: (i, half))] + w_specs
        args = (h2, zs1, zs2, wcat_bf, cw, cb)
        zt_shape, zt_spec = (m, fh), pl.BlockSpec((tm, fh), lambda i: (i, 0))
        block_bytes = _nbytes((tm, d), BF16) + _nbytes((tm, fh), BF16) + 3 * _nbytes((tm, fh), F32)
    return pl.pallas_call(
        functools.partial(_ffn_up_kernel, tm=tm, seq_len=seq_len, streaming=streaming),
        grid=(m // tm,),
        in_specs=in_specs,
        out_specs=(pl.BlockSpec((tm, fh), lambda i: (i, 0)), zt_spec),
        out_shape=(jax.ShapeDtypeStruct((m, fh), BF16), jax.ShapeDtypeStruct(zt_shape, F32)),
        scratch_shapes=scratch,
        compiler_params=pltpu.CompilerParams(
            dimension_semantics=("arbitrary",),
            vmem_limit_bytes=_vmem_limit(block_bytes, scratch_bytes)),
        name="conv_glu_up",
    )(*args)


def _ffn_down(x1, act0, act1, wo_bf, *, tm):
    m, d = x1.shape
    fh = act0.shape[1]
    resident = pl.Buffered(1)
    block_bytes = 2 * _nbytes((tm, d), F32) + 2 * _nbytes((tm, fh), BF16)
    return pl.pallas_call(
        _ffn_down_kernel,
        grid=(m // tm,),
        in_specs=[
            pl.BlockSpec((tm, d), lambda i: (i, 0)),
            pl.BlockSpec((tm, fh), lambda i: (i, 0)),
            pl.BlockSpec((tm, fh), lambda i: (i, 0)),
            pl.BlockSpec((fh, d), lambda i: (0, 0), pipeline_mode=resident),
            pl.BlockSpec((fh, d), lambda i: (1, 0), pipeline_mode=resident),
        ],
        out_specs=pl.BlockSpec((tm, d), lambda i: (i, 0)),
        out_shape=jax.ShapeDtypeStruct((m, d), F32),
        compiler_params=pltpu.CompilerParams(
            dimension_semantics=("arbitrary",),
            vmem_limit_bytes=_vmem_limit(block_bytes, 2 * _nbytes((fh, d), BF16))),
        name="conv_glu_down",
    )(x1, act0, act1, wo_bf, wo_bf)


def _ffn(x1, h2, wcat_bf, cw, cb, wo_bf, *, tm, seq_len, zs1=None, zs2=None):
    ups = [_ffn_up(h2, wcat_bf, cw, cb, half=half, tm=tm, seq_len=seq_len, zs1=zs1, zs2=zs2)
           for half in range(2)]
    y = _ffn_down(x1, ups[0][0], ups[1][0], wo_bf, tm=tm)
    return y, jnp.concatenate([ups[0][1], ups[1][1]], axis=1)


def _prep_weights(attn_norm_g, w_in, q_norm_g, k_norm_g, lambda_q1, lambda_k1, lambda_q2, lambda_k2,
                  subln_g, w_pool, pool_scale, w_up_attn, w_up_pool, w_out,
                  ffn_norm_g, w_ffn_in, conv_w, conv_b, w_ffn_out, *, n_heads):
    d_ff = conv_w.shape[1]
    fp = -(-d_ff // FF_TILE) * FF_TILE
    pad = fp - d_ff
    lam_vec = jnp.pad(jnp.stack([lambda_q1, lambda_k1, lambda_q2, lambda_k2]).astype(F32),
                      ((0, 0), (0, LANES - lambda_q1.shape[0])))
    group = np.arange(MXU_TILE) // QK_HALF
    gd = jnp.asarray((group[:, None] == group[None, :]).astype(np.float32) / QK_HALF, BF16)
    score_bound = (QK_HALF ** 0.5 * BOUND_MARGIN * jnp.max(jnp.abs(q_norm_g), axis=1)
                   * jnp.max(jnp.abs(k_norm_g), axis=1)).astype(F32)
    use_bound = (jnp.max(score_bound) <= MAX_FIXED_SHIFT).astype(jnp.int32).reshape(1)
    w_ffn_cat = _interleave_ffn_in(w_ffn_in, d_ff=d_ff, fp=fp, rows=CAST_ROWS_WIDE)
    return dict(
        score_bound=score_bound,
        use_bound=use_bound,
        attn_g=attn_norm_g.reshape(1, -1),
        w_in=_to_bf16(w_in, rows=CAST_ROWS_WIDE),
        qg=(jnp.tile(q_norm_g.reshape(-1), n_heads) * (QK_HALF ** -0.5)).reshape(1, -1),
        kg=jnp.tile(k_norm_g.reshape(-1), n_heads).reshape(1, -1),
        gd=gd,
        lam_vec=lam_vec,
        sg=subln_g.reshape(1, -1),
        w_pool=w_pool.astype(BF16),
        pool_scale=pool_scale.reshape(1, -1),
        w_up_attn=_to_bf16(w_up_attn, rows=CAST_ROWS),
        w_up_pool=_to_bf16(w_up_pool, rows=CAST_ROWS),
        w_out=_to_bf16(w_out, rows=CAST_ROWS),
        ffn_g=ffn_norm_g.reshape(1, -1),
        w_ffn_cat=w_ffn_cat,
        conv_w=jnp.pad(conv_w, ((0, 0), (0, pad))),
        conv_b=jnp.pad(conv_b, (0, pad)).reshape(1, -1),
        w_ffn_out=_pad_rows_bf16(w_ffn_out, fp=fp, rows=LANES),
        d_ff=d_ff,
    )


def _layer(x, start, caches, layer, pool_hist, conv_hist, lam_init, w, *, n_heads, tm, tq):
    n_seq, t_len, d = x.shape
    m = n_seq * t_len
    d_ff = w["d_ff"]
    x2 = x.reshape(m, d)

    q, k_f32, kb, v_f32, vb, u, sig = _in_proj(
        x2, w["attn_g"], w["w_in"], w["qg"], w["kg"], w["gd"], tm=min(tm, INPROJ_ROWS), n_heads=n_heads)

    if caches is None:
        o = _attn_prompt(q, kb, vb, w["lam_vec"], w["sg"], w["score_bound"], w["use_bound"],
                         tq=tq, tk=tq, lam_init=lam_init)
    else:
        o = _attn_sample(q, kb, vb, caches[0], caches[1], w["lam_vec"], w["sg"],
                         layer=layer, t_new=t_len, lam_init=lam_init)

    hist = jnp.pad(pool_hist, ((0, 0), (HIST_ROWS - POOL_HIST, 0), (0, 0))).reshape(n_seq * HIST_ROWS, -1)
    pool_tm = min(tm, t_len)
    pb = _pool(u, hist, w["w_pool"], w["pool_scale"], n_seq=n_seq, seq_len=t_len, tm=pool_tm, start=start)

    x1, h2 = _merge(o, pb, sig, x2, w["w_up_attn"], w["w_up_pool"], w["w_out"], w["ffn_g"], tm=tm)

    ffn_tm = min(tm, FFN_ROWS)
    if conv_hist is None:
        y, z_tail = _ffn(x1, h2, w["w_ffn_cat"], w["conv_w"], w["conv_b"], w["w_ffn_out"],
                         tm=ffn_tm, seq_len=t_len)
        conv_state = z_tail[-(CONV_WIDTH - 1):, :d_ff][None]
    else:
        fpad = w["conv_b"].shape[1] - d_ff
        zs1 = jnp.pad(conv_hist[:, 1:2], ((0, 0), (0, t_len - 1), (0, fpad))).reshape(m, -1)
        zs2 = jnp.pad(conv_hist, ((0, 0), (0, t_len - (CONV_WIDTH - 1)), (0, fpad))).reshape(m, -1)
        y, z_all = _ffn(x1, h2, w["w_ffn_cat"], w["conv_w"], w["conv_b"], w["w_ffn_out"],
                        tm=ffn_tm, seq_len=t_len, zs1=zs1, zs2=zs2)
        conv_state = z_all.reshape(n_seq, t_len, -1)[:, -(CONV_WIDTH - 1):, :d_ff]

    u3 = u.reshape(n_seq, t_len, -1)
    pool_state = jnp.concatenate([pool_hist, u3], axis=1)[:, -POOL_HIST:] if t_len < POOL_HIST else u3[:, -POOL_HIST:]
    return (y.reshape(n_seq, t_len, d),
            k_f32.reshape(n_seq, t_len, n_heads, HEAD_W),
            v_f32.reshape(n_seq, t_len, n_heads, HEAD_W),
            pool_state, conv_state)


def kernel(x_prompt, x_sample, cache_k, cache_v, state_pool, state_conv, attn_norm_g, w_in, q_norm_g, k_norm_g, lambda_q1, lambda_k1, lambda_q2, lambda_k2, subln_g, w_pool, pool_scale, w_up_attn, w_up_pool, w_out, ffn_norm_g, w_ffn_in, conv_w, conv_b, w_ffn_out):
    depth = w_in.shape[0]
    n_heads = cache_k.shape[3]
    past = cache_k.shape[2]
    y_p, y_s = x_prompt, x_sample
    outs = [[] for _ in range(8)]
    for l in range(depth):
        lam_init = 0.8 - 0.6 * math.exp(-0.3 * l)
        w = _prep_weights(attn_norm_g[l], w_in[l], q_norm_g[l], k_norm_g[l], lambda_q1[l], lambda_k1[l],
                          lambda_q2[l], lambda_k2[l], subln_g[l], w_pool[l], pool_scale[l], w_up_attn[l],
                          w_up_pool[l], w_out[l], ffn_norm_g[l], w_ffn_in[l], conv_w[l], conv_b[l],
                          w_ffn_out[l], n_heads=n_heads)
        pool0 = jnp.zeros((x_prompt.shape[0], POOL_HIST, state_pool.shape[-1]), F32)
        y_p, kp, vp, pp, cp = _layer(y_p, 0, None, l, pool0, None, lam_init, w,
                                     n_heads=n_heads, tm=512, tq=512)
        y_s, ks, vs, ps, cs = _layer(y_s, past, (cache_k, cache_v), l, state_pool[l], state_conv[l],
                                     lam_init, w, n_heads=n_heads,
                                     tm=x_sample.shape[0] * x_sample.shape[1], tq=None)
        for lst, val in zip(outs, (kp, vp, pp, cp, ks, vs, ps, cs)):
            lst.append(val)
    return (y_p, y_s) + tuple(jnp.stack(lst) for lst in outs)
```

```python
import functools
import math

import numpy as np
import jax
import jax.numpy as jnp
from jax import lax
from jax.experimental import pallas as pl
from jax.experimental.pallas import tpu as pltpu

F32 = jnp.float32
BF16 = jnp.bfloat16

EPS = 1e-6
NEG_INF = -1e30
CHUNK = 64
POOL_WINDOWS = (2, 4, 8, 16)
POOL_HIST = max(POOL_WINDOWS) - 1
CONV_WIDTH = 3

LANES = 128
HIST_ROWS = 16
MXU_TILE = 256
VMEM_LIMIT_CAP = 60 * 1024 * 1024
VMEM_TEMP_ALLOWANCE = 12 * 1024 * 1024

HEAD_W = 128
QK_HALF = 64
FF_TILE = 512
SAMPLE_POS_CHUNK = 512
INPROJ_ROWS = 256
FFN_ROWS = 512
CAST_ROWS = 512
CAST_ROWS_WIDE = 128
MAX_FIXED_SHIFT = 40.0
BOUND_MARGIN = 1.02


def _vmem_limit(block_bytes, scratch_bytes):
    est = 2 * block_bytes + scratch_bytes + VMEM_TEMP_ALLOWANCE
    return int(min(est, VMEM_LIMIT_CAP))


def _nbytes(shape, dtype):
    return int(np.prod(shape)) * jnp.dtype(dtype).itemsize


def _rms_rows(x, g):
    ms = jnp.mean(x * x, axis=-1, keepdims=True)
    return x * lax.rsqrt(ms + EPS) * g


def _dot(a, b):
    return jnp.dot(a, b, preferred_element_type=F32)


def _dot_nt(a, b):
    return lax.dot_general(a, b, (((1,), (1,)), ((), ())), preferred_element_type=F32)


def _cast_kernel(x_ref, o_ref):
    o_ref[...] = x_ref[...].astype(BF16)


def _to_bf16(w, *, rows):
    r, c = w.shape
    return pl.pallas_call(
        _cast_kernel,
        grid=(r // rows,),
        in_specs=[pl.BlockSpec((rows, c), lambda i: (i, 0))],
        out_specs=pl.BlockSpec((rows, c), lambda i: (i, 0)),
        out_shape=jax.ShapeDtypeStruct((r, c), BF16),
        compiler_params=pltpu.CompilerParams(
            dimension_semantics=("arbitrary",),
            vmem_limit_bytes=_vmem_limit(_nbytes((rows, c), F32) + _nbytes((rows, c), BF16), 0)),
        name="cast_bf16",
    )(w)


def _interleave_cast_kernel(x_ref, o_ref, *, d_ff):
    rows = x_ref.shape[0]
    for n in range(o_ref.shape[1] // (2 * MXU_TILE)):
        lo = n * MXU_TILE
        width = max(0, min(MXU_TILE, d_ff - lo))
        for part, src in ((0, lo), (1, d_ff + lo)):
            dst = (2 * n + part) * MXU_TILE
            if width:
                o_ref[:, dst:dst + width] = x_ref[:, src:src + width].astype(BF16)
            if width < MXU_TILE:
                o_ref[:, dst + width:dst + MXU_TILE] = jnp.zeros((rows, MXU_TILE - width), BF16)


def _interleave_ffn_in(w, *, d_ff, fp, rows):
    d = w.shape[0]
    return pl.pallas_call(
        functools.partial(_interleave_cast_kernel, d_ff=d_ff),
        grid=(d // rows,),
        in_specs=[pl.BlockSpec((rows, 2 * d_ff), lambda i: (i, 0))],
        out_specs=pl.BlockSpec((rows, 2 * fp), lambda i: (i, 0)),
        out_shape=jax.ShapeDtypeStruct((d, 2 * fp), BF16),
        compiler_params=pltpu.CompilerParams(
            dimension_semantics=("arbitrary",),
            vmem_limit_bytes=_vmem_limit(_nbytes((rows, 2 * d_ff), F32) + _nbytes((rows, 2 * fp), BF16), 0)),
        name="interleave_cast_ffn_in",
    )(w)


def _pad_cast_kernel(x_ref, o_ref, *, n_in):
    keep = pl.program_id(0) < n_in
    o_ref[...] = jnp.where(keep, x_ref[...], 0.0).astype(BF16)


def _pad_rows_bf16(w, *, fp, rows):
    d_ff, d = w.shape
    n_in = d_ff // rows
    return pl.pallas_call(
        functools.partial(_pad_cast_kernel, n_in=n_in),
        grid=(fp // rows,),
        in_specs=[pl.BlockSpec((rows, d), lambda i: (jnp.minimum(i, n_in - 1), 0))],
        out_specs=pl.BlockSpec((rows, d), lambda i: (i, 0)),
        out_shape=jax.ShapeDtypeStruct((fp, d), BF16),
        compiler_params=pltpu.CompilerParams(
            dimension_semantics=("arbitrary",),
            vmem_limit_bytes=_vmem_limit(_nbytes((rows, d), F32) + _nbytes((rows, d), BF16), 0)),
        name="pad_cast_ffn_out",
    )(w)


def _inproj_kernel(x_ref, g_ref, w_ref, qg_ref, kg_ref, gd_ref,
                   q_ref, kf_ref, kb_ref, vf_ref, vb_ref, u_ref, sig_ref, h_scr,
                   *, n_heads):
    tm = x_ref.shape[0]
    aw = n_heads * HEAD_W
    n_col = w_ref.shape[1] // aw
    h_scr[...] = _rms_rows(x_ref[...], g_ref[...]).astype(BF16)

    def col_tile(j):
        return _dot(h_scr[...], w_ref[:, j * aw:(j + 1) * aw])

    def qk_norm(y, gain_ref):
        heads = []
        for c in range(aw // MXU_TILE):
            cs = slice(c * MXU_TILE, (c + 1) * MXU_TILE)
            yc = y[:, cs]
            ms = _dot((yc * yc).astype(BF16), gd_ref[...])
            yn = yc * lax.rsqrt(ms + EPS) * gain_ref[:, cs]
            heads += [yn[:, :HEAD_W], yn[:, HEAD_W:]]
        return heads

    for j in range(4, n_col):
        sig_ref[:, (j - 4) * aw:(j - 3) * aw] = jax.nn.sigmoid(col_tile(j)).astype(BF16)

    feat = lax.broadcasted_iota(jnp.int32, (tm, HEAD_W), 1)
    for h, yh in enumerate(qk_norm(col_tile(0), qg_ref)):
        q_ref[h, 0] = jnp.where(feat < QK_HALF, yh, 0.0).astype(BF16)
        q_ref[h, 1] = jnp.where(feat >= QK_HALF, yh, 0.0).astype(BF16)

    for h, yh in enumerate(qk_norm(col_tile(1), kg_ref)):
        kf_ref[:, h * HEAD_W:(h + 1) * HEAD_W] = yh
        kb_ref[h] = yh.astype(BF16)

    y = col_tile(2)
    vf_ref[...] = y
    for h in range(n_heads):
        vb_ref[h] = y[:, h * HEAD_W:(h + 1) * HEAD_W].astype(BF16)

    u_ref[...] = col_tile(3)


def _in_proj(x, g, w_bf, qg, kg, gd, *, tm, n_heads):
    m, d = x.shape
    aw = n_heads * HEAD_W
    n_all = w_bf.shape[1]
    gate_w = n_all - 4 * aw
    q_shape, q_spec = (n_heads, 2, m, HEAD_W), pl.BlockSpec((n_heads, 2, tm, HEAD_W), lambda i: (0, 0, i, 0))
    v_shape, v_spec = (n_heads, m, HEAD_W), pl.BlockSpec((n_heads, tm, HEAD_W), lambda i: (0, i, 0))
    out_shape = (
        jax.ShapeDtypeStruct(q_shape, BF16),
        jax.ShapeDtypeStruct((m, aw), F32),
        jax.ShapeDtypeStruct((n_heads, m, HEAD_W), BF16),
        jax.ShapeDtypeStruct((m, aw), F32),
        jax.ShapeDtypeStruct(v_shape, BF16),
        jax.ShapeDtypeStruct((m, aw), F32),
        jax.ShapeDtypeStruct((m, gate_w), BF16),
    )
    resident = pl.Buffered(1)
    in_specs = [
        pl.BlockSpec((tm, d), lambda i: (i, 0)),
        pl.BlockSpec((1, d), lambda i: (0, 0)),
        pl.BlockSpec((d, n_all), lambda i: (0, 0), pipeline_mode=resident),
        pl.BlockSpec((1, aw), lambda i: (0, 0)),
        pl.BlockSpec((1, aw), lambda i: (0, 0)),
        pl.BlockSpec((MXU_TILE, MXU_TILE), lambda i: (0, 0)),
    ]
    out_specs = (
        q_spec,
        pl.BlockSpec((tm, aw), lambda i: (i, 0)),
        pl.BlockSpec((n_heads, tm, HEAD_W), lambda i: (0, i, 0)),
        pl.BlockSpec((tm, aw), lambda i: (i, 0)),
        v_spec,
        pl.BlockSpec((tm, aw), lambda i: (i, 0)),
        pl.BlockSpec((tm, gate_w), lambda i: (i, 0)),
    )
    block_bytes = (_nbytes((tm, d), F32) + 3 * _nbytes((tm, aw), F32) + 4 * _nbytes((tm, aw), BF16)
                   + _nbytes((tm, gate_w), BF16))
    return pl.pallas_call(
        functools.partial(_inproj_kernel, n_heads=n_heads),
        grid=(m // tm,),
        in_specs=in_specs,
        out_specs=out_specs,
        out_shape=out_shape,
        scratch_shapes=[pltpu.VMEM((tm, d), BF16)],
        compiler_params=pltpu.CompilerParams(
            dimension_semantics=("arbitrary",),
            vmem_limit_bytes=_vmem_limit(block_bytes, _nbytes((d, n_all), BF16) + _nbytes((tm, d), BF16))),
        name="in_proj",
    )(x, g, w_bf, qg, kg, gd)


def _lambda_value(lam_ref, lam_init):
    a = lam_ref[...]
    s1 = jnp.sum(a[0:1] * a[1:2], axis=1, keepdims=True)
    s2 = jnp.sum(a[2:3] * a[3:4], axis=1, keepdims=True)
    return jnp.exp(s1) - jnp.exp(s2) + lam_init


def _sub_ln(o, sg, lam_init):
    o = o * lax.rsqrt(jnp.mean(o * o, axis=-1, keepdims=True) + EPS)
    return o * sg * (1.0 - lam_init)


def _attn_kernel(qi_ref, kj_ref, flag_ref, fast_ref, bound_ref, lam_ref, sg_ref, q_ref, k_ref, v_ref, o_ref,
                 m_scr, l_scr, acc_scr, *, tq, tk, n_heads, lam_init):
    t = pl.program_id(0)
    i = qi_ref[t]
    j = kj_ref[t]
    flags = flag_ref[t]
    is_first = (flags & 1) != 0
    is_last = (flags & 2) != 0
    is_masked = (flags & 4) != 0
    fast = fast_ref[0] != 0

    @pl.when(is_first)
    def _():
        m_scr[...] = jnp.full(m_scr.shape, NEG_INF, F32)
        l_scr[...] = jnp.zeros(l_scr.shape, F32)
        acc_scr[...] = jnp.zeros(acc_scr.shape, F32)

    def step(masked, bounded):
        if masked:
            qpos = i * tq + lax.broadcasted_iota(jnp.int32, (tq, tk), 0)
            kpos = j * tk + lax.broadcasted_iota(jnp.int32, (tq, tk), 1)
            mask = (kpos // CHUNK) <= (qpos // CHUNK)

        def head(h, carry):
            kh = k_ref[h]
            vh = v_ref[h]
            for c in range(2):
                s = _dot_nt(q_ref[h, c], kh)
                if bounded:
                    s = s - bound_ref[c]
                if masked:
                    s = jnp.where(mask, s, NEG_INF)
                if bounded:
                    p = jnp.exp(s)
                    psum = p[:, :LANES]
                    for n in range(1, tk // LANES):
                        psum = psum + p[:, n * LANES:(n + 1) * LANES]
                    l_scr[h, c] += psum
                    acc_scr[h, c] += _dot(p.astype(BF16), vh)
                else:
                    m_prev = m_scr[h, c]
                    m_next = jnp.maximum(m_prev, jnp.max(s, axis=1, keepdims=True))
                    p = jnp.exp(s - jnp.tile(m_next, (1, tk // LANES)))
                    alpha = jnp.exp(m_prev - m_next)
                    l_scr[h, c] = alpha * l_scr[h, c] + jnp.sum(p, axis=1, keepdims=True)
                    m_scr[h, c] = m_next
                    acc_scr[h, c] = alpha * acc_scr[h, c] + _dot(p.astype(BF16), vh)
            return carry

        lax.fori_loop(0, n_heads, head, 0, unroll=bounded)

    for masked in (False, True):
        for bounded in (False, True):
            pl.when((is_masked == masked) & (fast == bounded))(functools.partial(step, masked, bounded))

    def finalize(bounded):
        lam = _lambda_value(lam_ref, lam_init)

        def row_sum(h, c):
            l = l_scr[h, c]
            return jnp.sum(l, axis=1, keepdims=True) if bounded else l

        def head(h, carry):
            o = acc_scr[h, 0] / row_sum(h, 0) - lam * (acc_scr[h, 1] / row_sum(h, 1))
            o_ref[h] = _sub_ln(o, sg_ref[...], lam_init).astype(BF16)
            return carry

        lax.fori_loop(0, n_heads, head, 0)

    for bounded in (False, True):
        pl.when(is_last & (fast == bounded))(functools.partial(finalize, bounded))


def _attn_schedule(seq, tq, tk):
    qi, kj, flags = [], [], []
    for i in range(seq // tq):
        q_lo, q_hi = i * tq, (i + 1) * tq - 1
        js = [j for j in range(seq // tk) if (j * tk) // CHUNK <= q_hi // CHUNK]
        for n, j in enumerate(js):
            masked = ((j + 1) * tk - 1) // CHUNK > q_lo // CHUNK
            qi.append(i)
            kj.append(j)
            flags.append((1 if n == 0 else 0) | (2 if n == len(js) - 1 else 0) | (4 if masked else 0))
    return (np.asarray(qi, np.int32), np.asarray(kj, np.int32), np.asarray(flags, np.int32))


def _attn_prompt(q, kb, vb, lam_vec, sg, score_bound, use_bound, *, tq, tk, lam_init):
    n_heads, _, seq, _ = q.shape
    qi, kj, flags = _attn_schedule(seq, tq, tk)
    stat = (n_heads, 2, tq, LANES)
    grid_spec = pltpu.PrefetchScalarGridSpec(
        num_scalar_prefetch=5,
        grid=(len(qi),),
        in_specs=[
            pl.BlockSpec((4, LANES), lambda t, qi, kj, *_: (0, 0)),
            pl.BlockSpec((1, HEAD_W), lambda t, qi, kj, *_: (0, 0)),
            pl.BlockSpec((n_heads, 2, tq, HEAD_W), lambda t, qi, kj, *_: (0, 0, qi[t], 0)),
            pl.BlockSpec((n_heads, tk, HEAD_W), lambda t, qi, kj, *_: (0, kj[t], 0)),
            pl.BlockSpec((n_heads, tk, HEAD_W), lambda t, qi, kj, *_: (0, kj[t], 0)),
        ],
        out_specs=pl.BlockSpec((n_heads, tq, HEAD_W), lambda t, qi, kj, *_: (0, qi[t], 0)),
        scratch_shapes=[pltpu.VMEM(stat, F32), pltpu.VMEM(stat, F32), pltpu.VMEM(stat, F32)],
    )
    block_bytes = (_nbytes((n_heads, 2, tq, HEAD_W), BF16) + 2 * _nbytes((n_heads, tk, HEAD_W), BF16)
                   + _nbytes((n_heads, tq, HEAD_W), BF16))
    return pl.pallas_call(
        functools.partial(_attn_kernel, tq=tq, tk=tk, n_heads=n_heads, lam_init=lam_init),
        grid_spec=grid_spec,
        out_shape=jax.ShapeDtypeStruct((n_heads, seq, HEAD_W), BF16),
        compiler_params=pltpu.CompilerParams(
            dimension_semantics=("arbitrary",),
            vmem_limit_bytes=_vmem_limit(block_bytes, 3 * _nbytes(stat, F32))),
        name="attn_prompt",
    )(jnp.asarray(qi), jnp.asarray(kj), jnp.asarray(flags), use_bound, score_bound, lam_vec, sg, q, kb, vb)


def _attn_dec_kernel(lam_ref, sg_ref, bias_c_ref, bias_n_ref, q_ref, kn_ref, vn_ref, ck_ref, cv_ref, o_ref,
                     *, pos_chunk, t_new, n_heads, lam_init):
    lam = _lambda_value(lam_ref, lam_init)
    past = ck_ref.shape[0]
    rows = n_heads * 2 * t_new
    q = q_ref[...].reshape(rows, HEAD_W)

    def fold(carry, k_flat, v_flat, bias):
        m, l, acc = carry
        s = _dot_nt(q, k_flat) + bias
        m_new = jnp.maximum(m, jnp.max(s, axis=1, keepdims=True))
        alpha = jnp.exp(m - m_new)
        p = jnp.exp(s - m_new)
        return (m_new, alpha * l + jnp.sum(p, axis=1, keepdims=True), alpha * acc + _dot(p.astype(BF16), v_flat))

    carry = (jnp.full((rows, 1), NEG_INF, F32), jnp.zeros((rows, 1), F32), jnp.zeros((rows, HEAD_W), F32))
    for c0 in range(0, past, pos_chunk):
        k_flat = ck_ref[c0:c0 + pos_chunk].reshape(pos_chunk * n_heads, HEAD_W).astype(BF16)
        v_flat = cv_ref[c0:c0 + pos_chunk].reshape(pos_chunk * n_heads, HEAD_W).astype(BF16)
        carry = fold(carry, k_flat, v_flat, bias_c_ref[...])
    carry = fold(carry, kn_ref[...].reshape(n_heads * t_new, HEAD_W), vn_ref[...].reshape(n_heads * t_new, HEAD_W),
                 bias_n_ref[...])
    _, l, acc = carry
    o_all = acc / l
    for h in range(n_heads):
        base = h * 2 * t_new
        o = o_all[base:base + t_new] - lam * o_all[base + t_new:base + 2 * t_new]
        o_ref[h] = _sub_ln(o, sg_ref[...], lam_init).astype(BF16)


def _sample_biases(n_heads, t_new, past, pos_chunk):
    rows = n_heads * 2 * t_new
    row_head = (np.arange(rows) // (2 * t_new))[:, None]
    q_chunk = ((past + np.arange(rows) % t_new) // CHUNK)[:, None]
    assert (past - 1) // CHUNK <= past // CHUNK
    col = np.arange(pos_chunk * n_heads)[None, :]
    bias_c = np.where(col % n_heads == row_head, 0.0, NEG_INF).astype(np.float32)
    col = np.arange(n_heads * t_new)[None, :]
    visible = (col // t_new == row_head) & ((past + col % t_new) // CHUNK <= q_chunk)
    bias_n = np.where(visible, 0.0, NEG_INF).astype(np.float32)
    return jnp.asarray(bias_c), jnp.asarray(bias_n)


def _attn_sample(q, kb, vb, cache_k, cache_v, lam_vec, sg, *, layer, t_new, lam_init):
    n_heads = q.shape[0]
    _, n_seq, past, _, _ = cache_k.shape
    aw = n_heads * HEAD_W
    pos_chunk = min(past, SAMPLE_POS_CHUNK)
    assert past % pos_chunk == 0
    bias_c, bias_n = _sample_biases(n_heads, t_new, past, pos_chunk)
    cache_spec = pl.BlockSpec((None, None, past, n_heads, HEAD_W), lambda b: (layer, b, 0, 0, 0))
    block_bytes = 2 * _nbytes((past, aw), F32) + 5 * _nbytes((n_heads, t_new, HEAD_W), BF16)
    return pl.pallas_call(
        functools.partial(_attn_dec_kernel, pos_chunk=pos_chunk, t_new=t_new, n_heads=n_heads,
                          lam_init=lam_init),
        grid=(n_seq,),
        in_specs=[
            pl.BlockSpec((4, LANES), lambda b: (0, 0)),
            pl.BlockSpec((1, HEAD_W), lambda b: (0, 0)),
            pl.BlockSpec(bias_c.shape, lambda b: (0, 0), pipeline_mode=pl.Buffered(1)),
            pl.BlockSpec(bias_n.shape, lambda b: (0, 0), pipeline_mode=pl.Buffered(1)),
            pl.BlockSpec((n_heads, 2, t_new, HEAD_W), lambda b: (0, 0, b, 0)),
            pl.BlockSpec((n_heads, t_new, HEAD_W), lambda b: (0, b, 0)),
            pl.BlockSpec((n_heads, t_new, HEAD_W), lambda b: (0, b, 0)),
            cache_spec,
            cache_spec,
        ],
        out_specs=pl.BlockSpec((n_heads, t_new, HEAD_W), lambda b: (0, b, 0)),
        out_shape=jax.ShapeDtypeStruct((n_heads, n_seq * t_new, HEAD_W), BF16),
        compiler_params=pltpu.CompilerParams(
            dimension_semantics=("arbitrary",),
            vmem_limit_bytes=_vmem_limit(block_bytes, _nbytes(bias_c.shape, F32) + _nbytes(bias_n.shape, F32))),
        name="attn_sample",
    )(lam_vec, sg, bias_c, bias_n, q, kb, vb, cache_k, cache_v)


def _pool_kernel(u_ref, hist_ref, wp_ref, ps_ref, pb_ref, ext_scr, *, tm, start):
    ti = pl.program_id(1)

    @pl.when(ti == 0)
    def _():
        ext_scr[0:HIST_ROWS] = hist_ref[...]

    @pl.when(ti > 0)
    def _():
        ext_scr[0:HIST_ROWS] = ext_scr[tm:tm + HIST_ROWS]

    ext_scr[HIST_ROWS:HIST_ROWS + tm] = u_ref[...]
    pos = start + ti * tm + lax.broadcasted_iota(jnp.int32, (tm, 1), 0)
    gw = wp_ref.shape[1]
    for g, w in enumerate(POOL_WINDOWS):
        cs = slice(g * gw, (g + 1) * gw)
        cur = ext_scr[HIST_ROWS:HIST_ROWS + tm, cs]
        win = cur
        for back in range(1, w):
            win = win + ext_scr[HIST_ROWS - back:HIST_ROWS - back + tm, cs]
        cnt = jnp.minimum(w, pos + 1).astype(F32)
        p = win / cnt - cur
        y = _dot(p.astype(BF16), wp_ref[g])
        pb_ref[:, cs] = (y * ps_ref[:, cs]).astype(BF16)


def _pool(u, hist, wp_bf, ps, *, n_seq, seq_len, tm, start):
    m, pw = u.shape
    tiles = seq_len // tm
    n_groups, gw, _ = wp_bf.shape
    block_bytes = (_nbytes((tm, pw), F32) + _nbytes((HIST_ROWS, pw), F32) + _nbytes(wp_bf.shape, BF16)
                   + _nbytes((tm, pw), BF16))
    return pl.pallas_call(
        functools.partial(_pool_kernel, tm=tm, start=start),
        grid=(n_seq, tiles),
        in_specs=[
            pl.BlockSpec((tm, pw), lambda b, t: (b * tiles + t, 0)),
            pl.BlockSpec((HIST_ROWS, pw), lambda b, t: (b, 0)),
            pl.BlockSpec((n_groups, gw, gw), lambda b, t: (0, 0, 0)),
            pl.BlockSpec((1, pw), lambda b, t: (0, 0)),
        ],
        out_specs=pl.BlockSpec((tm, pw), lambda b, t: (b * tiles + t, 0)),
        out_shape=jax.ShapeDtypeStruct((m, pw), BF16),
        scratch_shapes=[pltpu.VMEM((HIST_ROWS + tm, pw), F32)],
        compiler_params=pltpu.CompilerParams(
            dimension_semantics=("arbitrary", "arbitrary"),
            vmem_limit_bytes=_vmem_limit(block_bytes, _nbytes((HIST_ROWS + tm, pw), F32))),
        name="pool_mix",
    )(u, hist, wp_bf, ps)


def _merge_kernel(o_ref, pb_ref, sig_ref, x_ref, wua_ref, wup_ref, wout_ref, g2_ref, x1_ref, h2_ref, m_scr,
                  *, n_heads):
    d = x_ref.shape[1]
    o = jnp.concatenate([o_ref[h] for h in range(n_heads)], axis=1)
    pb = pb_ref[...]
    for n in range(d // FF_TILE):
        cs = slice(n * FF_TILE, (n + 1) * FF_TILE)
        gs = slice(d + n * FF_TILE, d + (n + 1) * FF_TILE)
        a = _dot(o, wua_ref[:, cs])
        b = _dot(pb, wup_ref[:, cs])
        m_scr[:, cs] = (sig_ref[:, cs].astype(F32) * a + sig_ref[:, gs].astype(F32) * b).astype(BF16)
    for n in range(d // FF_TILE):
        cs = slice(n * FF_TILE, (n + 1) * FF_TILE)
        x1_ref[:, cs] = x_ref[:, cs] + _dot(m_scr[...], wout_ref[:, cs])
    h2_ref[...] = _rms_rows(x1_ref[...], g2_ref[...]).astype(BF16)


def _merge(o, pb, sig, x, wua_bf, wup_bf, wout_bf, ffn_g, *, tm):
    n_heads = o.shape[0]
    m, d = x.shape
    aw = n_heads * HEAD_W
    block_bytes = (2 * _nbytes((tm, aw), BF16) + _nbytes((tm, 2 * d), BF16) + 2 * _nbytes((tm, d), F32)
                   + _nbytes((tm, d), BF16))
    weight_bytes = 2 * _nbytes((aw, d), BF16) + _nbytes((d, d), BF16)
    resident = pl.Buffered(1)
    return pl.pallas_call(
        functools.partial(_merge_kernel, n_heads=n_heads),
        grid=(m // tm,),
        in_specs=[
            pl.BlockSpec((n_heads, tm, HEAD_W), lambda i: (0, i, 0)),
            pl.BlockSpec((tm, aw), lambda i: (i, 0)),
            pl.BlockSpec((tm, 2 * d), lambda i: (i, 0)),
            pl.BlockSpec((tm, d), lambda i: (i, 0)),
            pl.BlockSpec((aw, d), lambda i: (0, 0), pipeline_mode=resident),
            pl.BlockSpec((aw, d), lambda i: (0, 0), pipeline_mode=resident),
            pl.BlockSpec((d, d), lambda i: (0, 0), pipeline_mode=resident),
            pl.BlockSpec((1, d), lambda i: (0, 0)),
        ],
        out_specs=(pl.BlockSpec((tm, d), lambda i: (i, 0)), pl.BlockSpec((tm, d), lambda i: (i, 0))),
        out_shape=(jax.ShapeDtypeStruct((m, d), F32), jax.ShapeDtypeStruct((m, d), BF16)),
        scratch_shapes=[pltpu.VMEM((tm, d), BF16)],
        compiler_params=pltpu.CompilerParams(
            dimension_semantics=("arbitrary",),
            vmem_limit_bytes=_vmem_limit(block_bytes, weight_bytes + _nbytes((tm, d), BF16))),
        name="merge_out_proj",
    )(o, pb, sig, x, wua_bf, wup_bf, wout_bf, ffn_g)


def _ffn_up_kernel(*refs, tm, seq_len, streaming):
    if streaming:
        h_ref, hp_ref, w_ref, cw_ref, cb_ref, act_ref, zt_ref, zext_scr, hext_scr = refs
        hext_scr[0:HIST_ROWS] = hp_ref[...]
        hext_scr[HIST_ROWS:HIST_ROWS + tm] = h_ref[...]
        lhs_ref = hext_scr
        not_first = pl.program_id(0) > 0
    else:
        h_ref, zs1_ref, zs2_ref, w_ref, cw_ref, cb_ref, act_ref, zt_ref, zext_scr = refs
        lhs_ref = h_ref
        r = lax.broadcasted_iota(jnp.int32, (tm, 1), 0) % seq_len
    for n in range(w_ref.shape[1] // (2 * MXU_TILE)):
        cs = slice(n * MXU_TILE, (n + 1) * MXU_TILE)
        zv = _dot(lhs_ref[...], w_ref[:, 2 * n * MXU_TILE:(2 * n + 2) * MXU_TILE])
        if streaming:
            front = jnp.where(not_first, zv[:HIST_ROWS, :MXU_TILE], 0.0)
            z = zv[HIST_ROWS:, :MXU_TILE]
            val = zv[HIST_ROWS:, MXU_TILE:]
            zext_scr[0:HIST_ROWS, cs] = front
            zext_scr[HIST_ROWS:HIST_ROWS + tm, cs] = z
            z1 = zext_scr[HIST_ROWS - 1:HIST_ROWS - 1 + tm, cs]
            z2 = zext_scr[HIST_ROWS - 2:HIST_ROWS - 2 + tm, cs]
            zt_ref[:, cs] = z[tm - 8:tm]
        else:
            z = zv[:, :MXU_TILE]
            val = zv[:, MXU_TILE:]
            zext_scr[0:HIST_ROWS, cs] = jnp.zeros((HIST_ROWS, MXU_TILE), F32)
            zext_scr[HIST_ROWS:HIST_ROWS + tm, cs] = z
            z1 = jnp.where(r >= 1, zext_scr[HIST_ROWS - 1:HIST_ROWS - 1 + tm, cs], 0.0) + zs1_ref[:, cs]
            z2 = jnp.where(r >= 2, zext_scr[HIST_ROWS - 2:HIST_ROWS - 2 + tm, cs], 0.0) + zs2_ref[:, cs]
            zt_ref[:, cs] = z
        zc = cb_ref[:, cs] + z2 * cw_ref[0:1, cs] + z1 * cw_ref[1:2, cs] + z * cw_ref[2:3, cs]
        act_ref[:, cs] = (jax.nn.silu(zc) * val).astype(BF16)


def _ffn_down_kernel(x_ref, a0_ref, a1_ref, w0_ref, w1_ref, y_ref):
    y_ref[...] = x_ref[...] + _dot(a0_ref[...], w0_ref[...]) + _dot(a1_ref[...], w1_ref[...])


def _ffn_up(h2, wcat_bf, cw, cb, *, half, tm, seq_len, zs1=None, zs2=None):
    m, d = h2.shape
    fh = wcat_bf.shape[1] // 4
    streaming = zs1 is None
    w_specs = [
        pl.BlockSpec((d, 2 * fh), lambda i: (0, half), pipeline_mode=pl.Buffered(1)),
        pl.BlockSpec((CONV_WIDTH, fh), lambda i: (0, half)),
        pl.BlockSpec((1, fh), lambda i: (0, half)),
    ]
    scratch = [pltpu.VMEM((HIST_ROWS + tm, fh), F32)]
    scratch_bytes = _nbytes((d, 2 * fh), BF16) + _nbytes((HIST_ROWS + tm, fh), F32)
    if streaming:
        assert seq_len == m and tm % HIST_ROWS == 0
        per_tile = tm // HIST_ROWS
        in_specs = [pl.BlockSpec((tm, d), lambda i: (i, 0)),
                    pl.BlockSpec((HIST_ROWS, d), lambda i: (jnp.maximum(i * per_tile - 1, 0), 0))] + w_specs
        args = (h2, h2, wcat_bf, cw, cb)
        zt_shape, zt_spec = (m // tm * 8, fh), pl.BlockSpec((8, fh), lambda i: (i, 0))
        block_bytes = _nbytes((tm, d), BF16) + _nbytes((tm, fh), BF16)
        scratch.append(pltpu.VMEM((HIST_ROWS + tm, d), BF16))
        scratch_bytes += _nbytes((HIST_ROWS + tm, d), BF16)
    else:
        assert tm % seq_len == 0 and seq_len >= CONV_WIDTH - 1
        in_specs = [pl.BlockSpec((tm, d), lambda i: (i, 0)),
                    pl.BlockSpec((tm, fh), lambda i: (i, half)),
                    pl.BlockSpec((tm, fh), lambda i: (i, half))] + w_specs
        args = (h2, zs1, zs2, wcat_bf, cw, cb)
        zt_shape, zt_spec = (m, fh), pl.BlockSpec((tm, fh), lambda i: (i, 0))
        block_bytes = _nbytes((tm, d), BF16) + _nbytes((tm, fh), BF16) + 3 * _nbytes((tm, fh), F32)
    return pl.pallas_call(
        functools.partial(_ffn_up_kernel, tm=tm, seq_len=seq_len, streaming=streaming),
        grid=(m // tm,),
        in_specs=in_specs,
        out_specs=(pl.BlockSpec((tm, fh), lambda i: (i, 0)), zt_spec),
        out_shape=(jax.ShapeDtypeStruct((m, fh), BF16), jax.ShapeDtypeStruct(zt_shape, F32)),
        scratch_shapes=scratch,
        compiler_params=pltpu.CompilerParams(
            dimension_semantics=("arbitrary",),
            vmem_limit_bytes=_vmem_limit(block_bytes, scratch_bytes)),
        name="conv_glu_up",
    )(*args)


def _ffn_down(x1, act0, act1, wo_bf, *, tm):
    m, d = x1.shape
    fh = act0.shape[1]
    resident = pl.Buffered(1)
    block_bytes = 2 * _nbytes((tm, d), F32) + 2 * _nbytes((tm, fh), BF16)
    return pl.pallas_call(
        _ffn_down_kernel,
        grid=(m // tm,),
        in_specs=[
            pl.BlockSpec((tm, d), lambda i: (i, 0)),
            pl.BlockSpec((tm, fh), lambda i: (i, 0)),
            pl.BlockSpec((tm, fh), lambda i: (i, 0)),
            pl.BlockSpec((fh, d), lambda i: (0, 0), pipeline_mode=resident),
            pl.BlockSpec((fh, d), lambda i: (1, 0), pipeline_mode=resident),
        ],
        out_specs=pl.BlockSpec((tm, d), lambda i: (i, 0)),
        out_shape=jax.ShapeDtypeStruct((m, d), F32),
        compiler_params=pltpu.CompilerParams(
            dimension_semantics=("arbitrary",),
            vmem_limit_bytes=_vmem_limit(block_bytes, 2 * _nbytes((fh, d), BF16))),
        name="conv_glu_down",
    )(x1, act0, act1, wo_bf, wo_bf)


def _ffn(x1, h2, wcat_bf, cw, cb, wo_bf, *, tm, seq_len, zs1=None, zs2=None):
    ups = [_ffn_up(h2, wcat_bf, cw, cb, half=half, tm=tm, seq_len=seq_len, zs1=zs1, zs2=zs2)
           for half in range(2)]
    y = _ffn_down(x1, ups[0][0], ups[1][0], wo_bf, tm=tm)
    return y, jnp.concatenate([ups[0][1], ups[1][1]], axis=1)


def _prep_weights(attn_norm_g, w_in, q_norm_g, k_norm_g, lambda_q1, lambda_k1, lambda_q2, lambda_k2,
                  subln_g, w_pool, pool_scale, w_up_attn, w_up_pool, w_out,
                  ffn_norm_g, w_ffn_in, conv_w, conv_b, w_ffn_out, *, n_heads):
    d_ff = conv_w.shape[1]
    fp = -(-d_ff // FF_TILE) * FF_TILE
    pad = fp - d_ff
    lam_vec = jnp.pad(jnp.stack([lambda_q1, lambda_k1, lambda_q2, lambda_k2]).astype(F32),
                      ((0, 0), (0, LANES - lambda_q1.shape[0])))
    group = np.arange(MXU_TILE) // QK_HALF
    gd = jnp.asarray((group[:, None] == group[None, :]).astype(np.float32) / QK_HALF, BF16)
    score_bound = (QK_HALF ** 0.5 * BOUND_MARGIN * jnp.max(jnp.abs(q_norm_g), axis=1)
                   * jnp.max(jnp.abs(k_norm_g), axis=1)).astype(F32)
    use_bound = (jnp.max(score_bound) <= MAX_FIXED_SHIFT).astype(jnp.int32).reshape(1)
    w_ffn_cat = _interleave_ffn_in(w_ffn_in, d_ff=d_ff, fp=fp, rows=CAST_ROWS_WIDE)
    return dict(
        score_bound=score_bound,
        use_bound=use_bound,
        attn_g=attn_norm_g.reshape(1, -1),
        w_in=_to_bf16(w_in, rows=CAST_ROWS_WIDE),
        qg=(jnp.tile(q_norm_g.reshape(-1), n_heads) * (QK_HALF ** -0.5)).reshape(1, -1),
        kg=jnp.tile(k_norm_g.reshape(-1), n_heads).reshape(1, -1),
        gd=gd,
        lam_vec=lam_vec,
        sg=subln_g.reshape(1, -1),
        w_pool=w_pool.astype(BF16),
        pool_scale=pool_scale.reshape(1, -1),
        w_up_attn=_to_bf16(w_up_attn, rows=CAST_ROWS),
        w_up_pool=_to_bf16(w_up_pool, rows=CAST_ROWS),
        w_out=_to_bf16(w_out, rows=CAST_ROWS),
        ffn_g=ffn_norm_g.reshape(1, -1),
        w_ffn_cat=w_ffn_cat,
        conv_w=jnp.pad(conv_w, ((0, 0), (0, pad))),
        conv_b=jnp.pad(conv_b, (0, pad)).reshape(1, -1),
        w_ffn_out=_pad_rows_bf16(w_ffn_out, fp=fp, rows=LANES),
        d_ff=d_ff,
    )


def _layer(x, start, caches, layer, pool_hist, conv_hist, lam_init, w, *, n_heads, tm, tq):
    n_seq, t_len, d = x.shape
    m = n_seq * t_len
    d_ff = w["d_ff"]
    x2 = x.reshape(m, d)

    q, k_f32, kb, v_f32, vb, u, sig = _in_proj(
        x2, w["attn_g"], w["w_in"], w["qg"], w["kg"], w["gd"], tm=min(tm, INPROJ_ROWS), n_heads=n_heads)

    if caches is None:
        o = _attn_prompt(q, kb, vb, w["lam_vec"], w["sg"], w["score_bound"], w["use_bound"],
                         tq=tq, tk=tq, lam_init=lam_init)
    else:
        o = _attn_sample(q, kb, vb, caches[0], caches[1], w["lam_vec"], w["sg"],
                         layer=layer, t_new=t_len, lam_init=lam_init)

    hist = jnp.pad(pool_hist, ((0, 0), (HIST_ROWS - POOL_HIST, 0), (0, 0))).reshape(n_seq * HIST_ROWS, -1)
    pool_tm = min(tm, t_len)
    pb = _pool(u, hist, w["w_pool"], w["pool_scale"], n_seq=n_seq, seq_len=t_len, tm=pool_tm, start=start)

    x1, h2 = _merge(o, pb, sig, x2, w["w_up_attn"], w["w_up_pool"], w["w_out"], w["ffn_g"], tm=tm)

    ffn_tm = min(tm, FFN_ROWS)
    if conv_hist is None:
        y, z_tail = _ffn(x1, h2, w["w_ffn_cat"], w["conv_w"], w["conv_b"], w["w_ffn_out"],
                         tm=ffn_tm, seq_len=t_len)
        conv_state = z_tail[-(CONV_WIDTH - 1):, :d_ff][None]
    else:
        fpad = w["conv_b"].shape[1] - d_ff
        zs1 = jnp.pad(conv_hist[:, 1:2], ((0, 0), (0, t_len - 1), (0, fpad))).reshape(m, -1)
        zs2 = jnp.pad(conv_hist, ((0, 0), (0, t_len - (CONV_WIDTH - 1)), (0, fpad))).reshape(m, -1)
        y, z_all = _ffn(x1, h2, w["w_ffn_cat"], w["conv_w"], w["conv_b"], w["w_ffn_out"],
                        tm=ffn_tm, seq_len=t_len, zs1=zs1, zs2=zs2)
        conv_state = z_all.reshape(n_seq, t_len, -1)[:, -(CONV_WIDTH - 1):, :d_ff]

    u3 = u.reshape(n_seq, t_len, -1)
    pool_state = jnp.concatenate([pool_hist, u3], axis=1)[:, -POOL_HIST:] if t_len < POOL_HIST else u3[:, -POOL_HIST:]
    return (y.reshape(n_seq, t_len, d),
            k_f32.reshape(n_seq, t_len, n_heads, HEAD_W),
            v_f32.reshape(n_seq, t_len, n_heads, HEAD_W),
            pool_state, conv_state)


def kernel(x_prompt, x_sample, cache_k, cache_v, state_pool, state_conv, attn_norm_g, w_in, q_norm_g, k_norm_g, lambda_q1, lambda_k1, lambda_q2, lambda_k2, subln_g, w_pool, pool_scale, w_up_attn, w_up_pool, w_out, ffn_norm_g, w_ffn_in, conv_w, conv_b, w_ffn_out):
    depth = w_in.shape[0]
    n_heads = cache_k.shape[3]
    past = cache_k.shape[2]
    y_p, y_s = x_prompt, x_sample
    outs = [[] for _ in range(8)]
    for l in range(depth):
        lam_init = 0.8 - 0.6 * math.exp(-0.3 * l)
        w = _prep_weights(attn_norm_g[l], w_in[l], q_norm_g[l], k_norm_g[l], lambda_q1[l], lambda_k1[l],
                          lambda_q2[l], lambda_k2[l], subln_g[l], w_pool[l], pool_scale[l], w_up_attn[l],
                          w_up_pool[l], w_out[l], ffn_norm_g[l], w_ffn_in[l], conv_w[l], conv_b[l],
                          w_ffn_out[l], n_heads=n_heads)
        pool0 = jnp.zeros((x_prompt.shape[0], POOL_HIST, state_pool.shape[-1]), F32)
        y_p, kp, vp, pp, cp = _layer(y_p, 0, None, l, pool0, None, lam_init, w,
                                     n_heads=n_heads, tm=512, tq=512)
        y_s, ks, vs, ps, cs = _layer(y_s, past, (cache_k, cache_v), l, state_pool[l], state_conv[l],
                                     lam_init, w, n_heads=n_heads,
                                     tm=x_sample.shape[0] * x_sample.shape[1], tq=None)
        for lst, val in zip(outs, (kp, vp, pp, cp, ks, vs, ps, cs)):
            lst.append(val)
    return (y_p, y_s) + tuple(jnp.stack(lst) for lst in outs)
```

```python
import functools
import math

import numpy as np
import jax
import jax.numpy as jnp
from jax import lax
from jax.experimental import pallas as pl
from jax.experimental.pallas import tpu as pltpu

F32 = jnp.float32
BF16 = jnp.bfloat16

EPS = 1e-6
NEG_INF = -1e30
CHUNK = 64
POOL_WINDOWS = (2, 4, 8, 16)
POOL_HIST = max(POOL_WINDOWS) - 1
CONV_WIDTH = 3

LANES = 128
HIST_ROWS = 16
MXU_TILE = 256
VMEM_LIMIT_CAP = 60 * 1024 * 1024
VMEM_TEMP_ALLOWANCE = 12 * 1024 * 1024

HEAD_W = 128
QK_HALF = 64
FF_TILE = 512
SAMPLE_POS_CHUNK = 512
INPROJ_ROWS = 256
FFN_ROWS = 512
CAST_ROWS = 512
CAST_ROWS_WIDE = 128
MAX_FIXED_SHIFT = 40.0
BOUND_MARGIN = 1.02


def _vmem_limit(block_bytes, scratch_bytes):
    est = 2 * block_bytes + scratch_bytes + VMEM_TEMP_ALLOWANCE
    return int(min(est, VMEM_LIMIT_CAP))


def _nbytes(shape, dtype):
    return int(np.prod(shape)) * jnp.dtype(dtype).itemsize


def _rms_rows(x, g):
    ms = jnp.mean(x * x, axis=-1, keepdims=True)
    return x * lax.rsqrt(ms + EPS) * g


def _dot(a, b):
    return jnp.dot(a, b, preferred_element_type=F32)


def _dot_nt(a, b):
    return lax.dot_general(a, b, (((1,), (1,)), ((), ())), preferred_element_type=F32)


def _cast_kernel(x_ref, o_ref):
    o_ref[...] = x_ref[...].astype(BF16)


def _to_bf16(w, *, rows):
    r, c = w.shape
    return pl.pallas_call(
        _cast_kernel,
        grid=(r // rows,),
        in_specs=[pl.BlockSpec((rows, c), lambda i: (i, 0))],
        out_specs=pl.BlockSpec((rows, c), lambda i: (i, 0)),
        out_shape=jax.ShapeDtypeStruct((r, c), BF16),
        compiler_params=pltpu.CompilerParams(
            dimension_semantics=("arbitrary",),
            vmem_limit_bytes=_vmem_limit(_nbytes((rows, c), F32) + _nbytes((rows, c), BF16), 0)),
        name="cast_bf16",
    )(w)


def _interleave_cast_kernel(x_ref, o_ref, *, d_ff):
    rows = x_ref.shape[0]
    for n in range(o_ref.shape[1] // (2 * MXU_TILE)):
        lo = n * MXU_TILE
        width = max(0, min(MXU_TILE, d_ff - lo))
        for part, src in ((0, lo), (1, d_ff + lo)):
            dst = (2 * n + part) * MXU_TILE
            if width:
                o_ref[:, dst:dst + width] = x_ref[:, src:src + width].astype(BF16)
            if width < MXU_TILE:
                o_ref[:, dst + width:dst + MXU_TILE] = jnp.zeros((rows, MXU_TILE - width), BF16)


def _interleave_ffn_in(w, *, d_ff, fp, rows):
    d = w.shape[0]
    return pl.pallas_call(
        functools.partial(_interleave_cast_kernel, d_ff=d_ff),
        grid=(d // rows,),
        in_specs=[pl.BlockSpec((rows, 2 * d_ff), lambda i: (i, 0))],
        out_specs=pl.BlockSpec((rows, 2 * fp), lambda i: (i, 0)),
        out_shape=jax.ShapeDtypeStruct((d, 2 * fp), BF16),
        compiler_params=pltpu.CompilerParams(
            dimension_semantics=("arbitrary",),
            vmem_limit_bytes=_vmem_limit(_nbytes((rows, 2 * d_ff), F32) + _nbytes((rows, 2 * fp), BF16), 0)),
        name="interleave_cast_ffn_in",
    )(w)


def _pad_cast_kernel(x_ref, o_ref, *, n_in):
    keep = pl.program_id(0) < n_in
    o_ref[...] = jnp.where(keep, x_ref[...], 0.0).astype(BF16)


def _pad_rows_bf16(w, *, fp, rows):
    d_ff, d = w.shape
    n_in = d_ff // rows
    return pl.pallas_call(
        functools.partial(_pad_cast_kernel, n_in=n_in),
        grid=(fp // rows,),
        in_specs=[pl.BlockSpec((rows, d), lambda i: (jnp.minimum(i, n_in - 1), 0))],
        out_specs=pl.BlockSpec((rows, d), lambda i: (i, 0)),
        out_shape=jax.ShapeDtypeStruct((fp, d), BF16),
        compiler_params=pltpu.CompilerParams(
            dimension_semantics=("arbitrary",),
            vmem_limit_bytes=_vmem_limit(_nbytes((rows, d), F32) + _nbytes((rows, d), BF16), 0)),
        name="pad_cast_ffn_out",
    )(w)


def _inproj_kernel(x_ref, g_ref, w_ref, qg_ref, kg_ref, gd_ref,
                   q_ref, kf_ref, kb_ref, vf_ref, vb_ref, u_ref, sig_ref, h_scr,
                   *, n_heads):
    tm = x_ref.shape[0]
    aw = n_heads * HEAD_W
    n_col = w_ref.shape[1] // aw
    h_scr[...] = _rms_rows(x_ref[...], g_ref[...]).astype(BF16)

    def col_tile(j):
        return _dot(h_scr[...], w_ref[:, j * aw:(j + 1) * aw])

    def qk_norm(y, gain_ref):
        heads = []
        for c in range(aw // MXU_TILE):
            cs = slice(c * MXU_TILE, (c + 1) * MXU_TILE)
            yc = y[:, cs]
            ms = _dot((yc * yc).astype(BF16), gd_ref[...])
            yn = yc * lax.rsqrt(ms + EPS) * gain_ref[:, cs]
            heads += [yn[:, :HEAD_W], yn[:, HEAD_W:]]
        return heads

    for j in range(4, n_col):
        sig_ref[:, (j - 4) * aw:(j - 3) * aw] = jax.nn.sigmoid(col_tile(j)).astype(BF16)

    feat = lax.broadcasted_iota(jnp.int32, (tm, HEAD_W), 1)
    for h, yh in enumerate(qk_norm(col_tile(0), qg_ref)):
        q_ref[h, 0] = jnp.where(feat < QK_HALF, yh, 0.0).astype(BF16)
        q_ref[h, 1] = jnp.where(feat >= QK_HALF, yh, 0.0).astype(BF16)

    for h, yh in enumerate(qk_norm(col_tile(1), kg_ref)):
        kf_ref[:, h * HEAD_W:(h + 1) * HEAD_W] = yh
        kb_ref[h] = yh.astype(BF16)

    y = col_tile(2)
    vf_ref[...] = y
    for h in range(n_heads):
        vb_ref[h] = y[:, h * HEAD_W:(h + 1) * HEAD_W].astype(BF16)

    u_ref[...] = col_tile(3)


def _in_proj(x, g, w_bf, qg, kg, gd, *, tm, n_heads):
    m, d = x.shape
    aw = n_heads * HEAD_W
    n_all = w_bf.shape[1]
    gate_w = n_all - 4 * aw
    q_shape, q_spec = (n_heads, 2, m, HEAD_W), pl.BlockSpec((n_heads, 2, tm, HEAD_W), lambda i: (0, 0, i, 0))
    v_shape, v_spec = (n_heads, m, HEAD_W), pl.BlockSpec((n_heads, tm, HEAD_W), lambda i: (0, i, 0))
    out_shape = (
        jax.ShapeDtypeStruct(q_shape, BF16),
        jax.ShapeDtypeStruct((m, aw), F32),
        jax.ShapeDtypeStruct((n_heads, m, HEAD_W), BF16),
        jax.ShapeDtypeStruct((m, aw), F32),
        jax.ShapeDtypeStruct(v_shape, BF16),
        jax.ShapeDtypeStruct((m, aw), F32),
        jax.ShapeDtypeStruct((m, gate_w), BF16),
    )
    resident = pl.Buffered(1)
    in_specs = [
        pl.BlockSpec((tm, d), lambda i: (i, 0)),
        pl.BlockSpec((1, d), lambda i: (0, 0)),
        pl.BlockSpec((d, n_all), lambda i: (0, 0), pipeline_mode=resident),
        pl.BlockSpec((1, aw), lambda i: (0, 0)),
        pl.BlockSpec((1, aw), lambda i: (0, 0)),
        pl.BlockSpec((MXU_TILE, MXU_TILE), lambda i: (0, 0)),
    ]
    out_specs = (
        q_spec,
        pl.BlockSpec((tm, aw), lambda i: (i, 0)),
        pl.BlockSpec((n_heads, tm, HEAD_W), lambda i: (0, i, 0)),
        pl.BlockSpec((tm, aw), lambda i: (i, 0)),
        v_spec,
        pl.BlockSpec((tm, aw), lambda i: (i, 0)),
        pl.BlockSpec((tm, gate_w), lambda i: (i, 0)),
    )
    block_bytes = (_nbytes((tm, d), F32) + 3 * _nbytes((tm, aw), F32) + 4 * _nbytes((tm, aw), BF16)
                   + _nbytes((tm, gate_w), BF16))
    return pl.pallas_call(
        functools.partial(_inproj_kernel, n_heads=n_heads),
        grid=(m // tm,),
        in_specs=in_specs,
        out_specs=out_specs,
        out_shape=out_shape,
        scratch_shapes=[pltpu.VMEM((tm, d), BF16)],
        compiler_params=pltpu.CompilerParams(
            dimension_semantics=("arbitrary",),
            vmem_limit_bytes=_vmem_limit(block_bytes, _nbytes((d, n_all), BF16) + _nbytes((tm, d), BF16))),
        name="in_proj",
    )(x, g, w_bf, qg, kg, gd)


def _lambda_value(lam_ref, lam_init):
    a = lam_ref[...]
    s1 = jnp.sum(a[0:1] * a[1:2], axis=1, keepdims=True)
    s2 = jnp.sum(a[2:3] * a[3:4], axis=1, keepdims=True)
    return jnp.exp(s1) - jnp.exp(s2) + lam_init


def _sub_ln(o, sg, lam_init):
    o = o * lax.rsqrt(jnp.mean(o * o, axis=-1, keepdims=True) + EPS)
    return o * sg * (1.0 - lam_init)


def _attn_kernel(qi_ref, kj_ref, flag_ref, fast_ref, bound_ref, lam_ref, sg_ref, q_ref, k_ref, v_ref, o_ref,
                 m_scr, l_scr, acc_scr, *, tq, tk, n_heads, lam_init):
    t = pl.program_id(0)
    i = qi_ref[t]
    j = kj_ref[t]
    flags = flag_ref[t]
    is_first = (flags & 1) != 0
    is_last = (flags & 2) != 0
    is_masked = (flags & 4) != 0
    fast = fast_ref[0] != 0

    @pl.when(is_first)
    def _():
        m_scr[...] = jnp.full(m_scr.shape, NEG_INF, F32)
        l_scr[...] = jnp.zeros(l_scr.shape, F32)
        acc_scr[...] = jnp.zeros(acc_scr.shape, F32)

    def step(masked, bounded):
        if masked:
            qpos = i * tq + lax.broadcasted_iota(jnp.int32, (tq, tk), 0)
            kpos = j * tk + lax.broadcasted_iota(jnp.int32, (tq, tk), 1)
            mask = (kpos // CHUNK) <= (qpos // CHUNK)

        def head(h, carry):
            kh = k_ref[h]
            vh = v_ref[h]
            for c in range(2):
                s = _dot_nt(q_ref[h, c], kh)
                if bounded:
                    s = s - bound_ref[c]
                if masked:
                    s = jnp.where(mask, s, NEG_INF)
                if bounded:
                    p = jnp.exp(s)
                    psum = p[:, :LANES]
                    for n in range(1, tk // LANES):
                        psum = psum + p[:, n * LANES:(n + 1) * LANES]
                    l_scr[h, c] += psum
                    acc_scr[h, c] += _dot(p.astype(BF16), vh)
                else:
                    m_prev = m_scr[h, c]
                    m_next = jnp.maximum(m_prev, jnp.max(s, axis=1, keepdims=True))
                    p = jnp.exp(s - jnp.tile(m_next, (1, tk // LANES)))
                    alpha = jnp.exp(m_prev - m_next)
                    l_scr[h, c] = alpha * l_scr[h, c] + jnp.sum(p, axis=1, keepdims=True)
                    m_scr[h, c] = m_next
                    acc_scr[h, c] = alpha * acc_scr[h, c] + _dot(p.astype(BF16), vh)
            return carry

        lax.fori_loop(0, n_heads, head, 0, unroll=bounded)

    for masked in (False, True):
        for bounded in (False, True):
            pl.when((is_masked == masked) & (fast == bounded))(functools.partial(step, masked, bounded))

    def finalize(bounded):
        lam = _lambda_value(lam_ref, lam_init)

        def row_sum(h, c):
            l = l_scr[h, c]
            return jnp.sum(l, axis=1, keepdims=True) if bounded else l

        def head(h, carry):
            o = acc_scr[h, 0] / row_sum(h, 0) - lam * (acc_scr[h, 1] / row_sum(h, 1))
            o_ref[h] = _sub_ln(o, sg_ref[...], lam_init).astype(BF16)
            return carry

        lax.fori_loop(0, n_heads, head, 0)

    for bounded in (False, True):
        pl.when(is_last & (fast == bounded))(functools.partial(finalize, bounded))


def _attn_schedule(seq, tq, tk):
    qi, kj, flags = [], [], []
    for i in range(seq // tq):
        q_lo, q_hi = i * tq, (i + 1) * tq - 1
        js = [j for j in range(seq // tk) if (j * tk) // CHUNK <= q_hi // CHUNK]
        for n, j in enumerate(js):
            masked = ((j + 1) * tk - 1) // CHUNK > q_lo // CHUNK
            qi.append(i)
            kj.append(j)
            flags.append((1 if n == 0 else 0) | (2 if n == len(js) - 1 else 0) | (4 if masked else 0))
    return (np.asarray(qi, np.int32), np.asarray(kj, np.int32), np.asarray(flags, np.int32))


def _attn_prompt(q, kb, vb, lam_vec, sg, score_bound, use_bound, *, tq, tk, lam_init):
    n_heads, _, seq, _ = q.shape
    qi, kj, flags = _attn_schedule(seq, tq, tk)
    stat = (n_heads, 2, tq, LANES)
    grid_spec = pltpu.PrefetchScalarGridSpec(
        num_scalar_prefetch=5,
        grid=(len(qi),),
        in_specs=[
            pl.BlockSpec((4, LANES), lambda t, qi, kj, *_: (0, 0)),
            pl.BlockSpec((1, HEAD_W), lambda t, qi, kj, *_: (0, 0)),
            pl.BlockSpec((n_heads, 2, tq, HEAD_W), lambda t, qi, kj, *_: (0, 0, qi[t], 0)),
            pl.BlockSpec((n_heads, tk, HEAD_W), lambda t, qi, kj, *_: (0, kj[t], 0)),
            pl.BlockSpec((n_heads, tk, HEAD_W), lambda t, qi, kj, *_: (0, kj[t], 0)),
        ],
        out_specs=pl.BlockSpec((n_heads, tq, HEAD_W), lambda t, qi, kj, *_: (0, qi[t], 0)),
        scratch_shapes=[pltpu.VMEM(stat, F32), pltpu.VMEM(stat, F32), pltpu.VMEM(stat, F32)],
    )
    block_bytes = (_nbytes((n_heads, 2, tq, HEAD_W), BF16) + 2 * _nbytes((n_heads, tk, HEAD_W), BF16)
                   + _nbytes((n_heads, tq, HEAD_W), BF16))
    return pl.pallas_call(
        functools.partial(_attn_kernel, tq=tq, tk=tk, n_heads=n_heads, lam_init=lam_init),
        grid_spec=grid_spec,
        out_shape=jax.ShapeDtypeStruct((n_heads, seq, HEAD_W), BF16),
        compiler_params=pltpu.CompilerParams(
            dimension_semantics=("arbitrary",),
            vmem_limit_bytes=_vmem_limit(block_bytes, 3 * _nbytes(stat, F32))),
        name="attn_prompt",
    )(jnp.asarray(qi), jnp.asarray(kj), jnp.asarray(flags), use_bound, score_bound, lam_vec, sg, q, kb, vb)


def _attn_dec_kernel(lam_ref, sg_ref, bias_c_ref, bias_n_ref, q_ref, kn_ref, vn_ref, ck_ref, cv_ref, o_ref,
                     *, pos_chunk, t_new, n_heads, lam_init):
    lam = _lambda_value(lam_ref, lam_init)
    past = ck_ref.shape[0]
    rows = n_heads * 2 * t_new
    q = q_ref[...].reshape(rows, HEAD_W)

    def fold(carry, k_flat, v_flat, bias):
        m, l, acc = carry
        s = _dot_nt(q, k_flat) + bias
        m_new = jnp.maximum(m, jnp.max(s, axis=1, keepdims=True))
        alpha = jnp.exp(m - m_new)
        p = jnp.exp(s - m_new)
        return (m_new, alpha * l + jnp.sum(p, axis=1, keepdims=True), alpha * acc + _dot(p.astype(BF16), v_flat))

    carry = (jnp.full((rows, 1), NEG_INF, F32), jnp.zeros((rows, 1), F32), jnp.zeros((rows, HEAD_W), F32))
    for c0 in range(0, past, pos_chunk):
        k_flat = ck_ref[c0:c0 + pos_chunk].reshape(pos_chunk * n_heads, HEAD_W).astype(BF16)
        v_flat = cv_ref[c0:c0 + pos_chunk].reshape(pos_chunk * n_heads, HEAD_W).astype(BF16)
        carry = fold(carry, k_flat, v_flat, bias_c_ref[...])
    carry = fold(carry, kn_ref[...].reshape(n_heads * t_new, HEAD_W), vn_ref[...].reshape(n_heads * t_new, HEAD_W),
                 bias_n_ref[...])
    _, l, acc = carry
    o_all = acc / l
    for h in range(n_heads):
        base = h * 2 * t_new
        o = o_all[base:base + t_new] - lam * o_all[base + t_new:base + 2 * t_new]
        o_ref[h] = _sub_ln(o, sg_ref[...], lam_init).astype(BF16)


def _sample_biases(n_heads, t_new, past, pos_chunk):
    rows = n_heads * 2 * t_new
    row_head = (np.arange(rows) // (2 * t_new))[:, None]
    q_chunk = ((past + np.arange(rows) % t_new) // CHUNK)[:, None]
    assert (past - 1) // CHUNK <= past // CHUNK
    col = np.arange(pos_chunk * n_heads)[None, :]
    bias_c = np.where(col % n_heads == row_head, 0.0, NEG_INF).astype(np.float32)
    col = np.arange(n_heads * t_new)[None, :]
    visible = (col // t_new == row_head) & ((past + col % t_new) // CHUNK <= q_chunk)
    bias_n = np.where(visible, 0.0, NEG_INF).astype(np.float32)
    return jnp.asarray(bias_c), jnp.asarray(bias_n)


def _attn_sample(q, kb, vb, cache_k, cache_v, lam_vec, sg, *, layer, t_new, lam_init):
    n_heads = q.shape[0]
    _, n_seq, past, _, _ = cache_k.shape
    aw = n_heads * HEAD_W
    pos_chunk = min(past, SAMPLE_POS_CHUNK)
    assert past % pos_chunk == 0
    bias_c, bias_n = _sample_biases(n_heads, t_new, past, pos_chunk)
    cache_spec = pl.BlockSpec((None, None, past, n_heads, HEAD_W), lambda b: (layer, b, 0, 0, 0))
    block_bytes = 2 * _nbytes((past, aw), F32) + 5 * _nbytes((n_heads, t_new, HEAD_W), BF16)
    return pl.pallas_call(
        functools.partial(_attn_dec_kernel, pos_chunk=pos_chunk, t_new=t_new, n_heads=n_heads,
                          lam_init=lam_init),
        grid=(n_seq,),
        in_specs=[
            pl.BlockSpec((4, LANES), lambda b: (0, 0)),
            pl.BlockSpec((1, HEAD_W), lambda b: (0, 0)),
            pl.BlockSpec(bias_c.shape, lambda b: (0, 0), pipeline_mode=pl.Buffered(1)),
            pl.BlockSpec(bias_n.shape, lambda b: (0, 0), pipeline_mode=pl.Buffered(1)),
            pl.BlockSpec((n_heads, 2, t_new, HEAD_W), lambda b: (0, 0, b, 0)),
            pl.BlockSpec((n_heads, t_new, HEAD_W), lambda b: (0, b, 0)),
            pl.BlockSpec((n_heads, t_new, HEAD_W), lambda b: (0, b, 0)),
            cache_spec,
            cache_spec,
        ],
        out_specs=pl.BlockSpec((n_heads, t_new, HEAD_W), lambda b: (0, b, 0)),
        out_shape=jax.ShapeDtypeStruct((n_heads, n_seq * t_new, HEAD_W), BF16),
        compiler_params=pltpu.CompilerParams(
            dimension_semantics=("arbitrary",),
            vmem_limit_bytes=_vmem_limit(block_bytes, _nbytes(bias_c.shape, F32) + _nbytes(bias_n.shape, F32))),
        name="attn_sample",
    )(lam_vec, sg, bias_c, bias_n, q, kb, vb, cache_k, cache_v)


def _pool_kernel(u_ref, hist_ref, wp_ref, ps_ref, pb_ref, ext_scr, *, tm, start):
    ti = pl.program_id(1)

    @pl.when(ti == 0)
    def _():
        ext_scr[0:HIST_ROWS] = hist_ref[...]

    @pl.when(ti > 0)
    def _():
        ext_scr[0:HIST_ROWS] = ext_scr[tm:tm + HIST_ROWS]

    ext_scr[HIST_ROWS:HIST_ROWS + tm] = u_ref[...]
    pos = start + ti * tm + lax.broadcasted_iota(jnp.int32, (tm, 1), 0)
    gw = wp_ref.shape[1]
    for g, w in enumerate(POOL_WINDOWS):
        cs = slice(g * gw, (g + 1) * gw)
        cur = ext_scr[HIST_ROWS:HIST_ROWS + tm, cs]
        win = cur
        for back in range(1, w):
            win = win + ext_scr[HIST_ROWS - back:HIST_ROWS - back + tm, cs]
        cnt = jnp.minimum(w, pos + 1).astype(F32)
        p = win / cnt - cur
        y = _dot(p.astype(BF16), wp_ref[g])
        pb_ref[:, cs] = (y * ps_ref[:, cs]).astype(BF16)


def _pool(u, hist, wp_bf, ps, *, n_seq, seq_len, tm, start):
    m, pw = u.shape
    tiles = seq_len // tm
    n_groups, gw, _ = wp_bf.shape
    block_bytes = (_nbytes((tm, pw), F32) + _nbytes((HIST_ROWS, pw), F32) + _nbytes(wp_bf.shape, BF16)
                   + _nbytes((tm, pw), BF16))
    return pl.pallas_call(
        functools.partial(_pool_kernel, tm=tm, start=start),
        grid=(n_seq, tiles),
        in_specs=[
            pl.BlockSpec((tm, pw), lambda b, t: (b * tiles + t, 0)),
            pl.BlockSpec((HIST_ROWS, pw), lambda b, t: (b, 0)),
            pl.BlockSpec((n_groups, gw, gw), lambda b, t: (0, 0, 0)),
            pl.BlockSpec((1, pw), lambda b, t: (0, 0)),
        ],
        out_specs=pl.BlockSpec((tm, pw), lambda b, t: (b * tiles + t, 0)),
        out_shape=jax.ShapeDtypeStruct((m, pw), BF16),
        scratch_shapes=[pltpu.VMEM((HIST_ROWS + tm, pw), F32)],
        compiler_params=pltpu.CompilerParams(
            dimension_semantics=("arbitrary", "arbitrary"),
            vmem_limit_bytes=_vmem_limit(block_bytes, _nbytes((HIST_ROWS + tm, pw), F32))),
        name="pool_mix",
    )(u, hist, wp_bf, ps)


def _merge_kernel(o_ref, pb_ref, sig_ref, x_ref, wua_ref, wup_ref, wout_ref, g2_ref, x1_ref, h2_ref, m_scr,
                  *, n_heads):
    d = x_ref.shape[1]
    o = jnp.concatenate([o_ref[h] for h in range(n_heads)], axis=1)
    pb = pb_ref[...]
    for n in range(d // FF_TILE):
        cs = slice(n * FF_TILE, (n + 1) * FF_TILE)
        gs = slice(d + n * FF_TILE, d + (n + 1) * FF_TILE)
        a = _dot(o, wua_ref[:, cs])
        b = _dot(pb, wup_ref[:, cs])
        m_scr[:, cs] = (sig_ref[:, cs].astype(F32) * a + sig_ref[:, gs].astype(F32) * b).astype(BF16)
    for n in range(d // FF_TILE):
        cs = slice(n * FF_TILE, (n + 1) * FF_TILE)
        x1_ref[:, cs] = x_ref[:, cs] + _dot(m_scr[...], wout_ref[:, cs])
    h2_ref[...] = _rms_rows(x1_ref[...], g2_ref[...]).astype(BF16)


def _merge(o, pb, sig, x, wua_bf, wup_bf, wout_bf, ffn_g, *, tm):
    n_heads = o.shape[0]
    m, d = x.shape
    aw = n_heads * HEAD_W
    block_bytes = (2 * _nbytes((tm, aw), BF16) + _nbytes((tm, 2 * d), BF16) + 2 * _nbytes((tm, d), F32)
                   + _nbytes((tm, d), BF16))
    weight_bytes = 2 * _nbytes((aw, d), BF16) + _nbytes((d, d), BF16)
    resident = pl.Buffered(1)
    return pl.pallas_call(
        functools.partial(_merge_kernel, n_heads=n_heads),
        grid=(m // tm,),
        in_specs=[
            pl.BlockSpec((n_heads, tm, HEAD_W), lambda i: (0, i, 0)),
            pl.BlockSpec((tm, aw), lambda i: (i, 0)),
            pl.BlockSpec((tm, 2 * d), lambda i: (i, 0)),
            pl.BlockSpec((tm, d), lambda i: (i, 0)),
            pl.BlockSpec((aw, d), lambda i: (0, 0), pipeline_mode=resident),
            pl.BlockSpec((aw, d), lambda i: (0, 0), pipeline_mode=resident),
            pl.BlockSpec((d, d), lambda i: (0, 0), pipeline_mode=resident),
            pl.BlockSpec((1, d), lambda i: (0, 0)),
        ],
        out_specs=(pl.BlockSpec((tm, d), lambda i: (i, 0)), pl.BlockSpec((tm, d), lambda i: (i, 0))),
        out_shape=(jax.ShapeDtypeStruct((m, d), F32), jax.ShapeDtypeStruct((m, d), BF16)),
        scratch_shapes=[pltpu.VMEM((tm, d), BF16)],
        compiler_params=pltpu.CompilerParams(
            dimension_semantics=("arbitrary",),
            vmem_limit_bytes=_vmem_limit(block_bytes, weight_bytes + _nbytes((tm, d), BF16))),
        name="merge_out_proj",
    )(o, pb, sig, x, wua_bf, wup_bf, wout_bf, ffn_g)


def _ffn_up_kernel(*refs, tm, seq_len, streaming):
    if streaming:
        h_ref, hp_ref, w_ref, cw_ref, cb_ref, act_ref, zt_ref, hext_scr = refs
        hext_scr[0:HIST_ROWS] = hp_ref[...]
        hext_scr[HIST_ROWS:HIST_ROWS + tm] = h_ref[...]
        lhs_ref = hext_scr
        not_first = pl.program_id(0) > 0
    else:
        h_ref, zs1_ref, zs2_ref, w_ref, cw_ref, cb_ref, act_ref, zt_ref = refs
        lhs_ref = h_ref
        r = lax.broadcasted_iota(jnp.int32, (tm, 1), 0) % seq_len
    for n in range(w_ref.shape[1] // (2 * MXU_TILE)):
        cs = slice(n * MXU_TILE, (n + 1) * MXU_TILE)
        zv = _dot(lhs_ref[...], w_ref[:, 2 * n * MXU_TILE:(2 * n + 2) * MXU_TILE])
        if streaming:
            row = lax.broadcasted_iota(jnp.int32, (HIST_ROWS + tm, 1), 0)
            zx = jnp.where((row >= HIST_ROWS) | not_first, zv[:, :MXU_TILE], 0.0)
            z = zx[HIST_ROWS:]
            val = zv[HIST_ROWS:, MXU_TILE:]
            z1 = pltpu.roll(zx, 1, axis=0)[HIST_ROWS:]
            z2 = pltpu.roll(zx, 2, axis=0)[HIST_ROWS:]
            zt_ref[:, cs] = z[tm - 8:tm]
        else:
            z = zv[:, :MXU_TILE]
            val = zv[:, MXU_TILE:]
            z1 = jnp.where(r >= 1, pltpu.roll(z, 1, axis=0), 0.0) + zs1_ref[:, cs]
            z2 = jnp.where(r >= 2, pltpu.roll(z, 2, axis=0), 0.0) + zs2_ref[:, cs]
            zt_ref[:, cs] = z
        zc = cb_ref[:, cs] + z2 * cw_ref[0:1, cs] + z1 * cw_ref[1:2, cs] + z * cw_ref[2:3, cs]
        act_ref[:, cs] = (jax.nn.silu(zc) * val).astype(BF16)


def _ffn_down_kernel(x_ref, a0_ref, a1_ref, w0_ref, w1_ref, y_ref):
    y_ref[...] = x_ref[...] + _dot(a0_ref[...], w0_ref[...]) + _dot(a1_ref[...], w1_ref[...])


def _ffn_up(h2, wcat_bf, cw, cb, *, half, tm, seq_len, zs1=None, zs2=None):
    m, d = h2.shape
    fh = wcat_bf.shape[1] // 4
    streaming = zs1 is None
    w_specs = [
        pl.BlockSpec((d, 2 * fh), lambda i: (0, half), pipeline_mode=pl.Buffered(1)),
        pl.BlockSpec((CONV_WIDTH, fh), lambda i: (0, half)),
        pl.BlockSpec((1, fh), lambda i: (0, half)),
    ]
    scratch = []
    scratch_bytes = _nbytes((d, 2 * fh), BF16)
    if streaming:
        assert seq_len == m and tm % HIST_ROWS == 0
        per_tile = tm // HIST_ROWS
        in_specs = [pl.BlockSpec((tm, d), lambda i: (i, 0)),
                    pl.BlockSpec((HIST_ROWS, d), lambda i: (jnp.maximum(i * per_tile - 1, 0), 0))] + w_specs
        args = (h2, h2, wcat_bf, cw, cb)
        zt_shape, zt_spec = (m // tm * 8, fh), pl.BlockSpec((8, fh), lambda i: (i, 0))
        block_bytes = _nbytes((tm, d), BF16) + _nbytes((tm, fh), BF16)
        scratch.append(pltpu.VMEM((HIST_ROWS + tm, d), BF16))
        scratch_bytes += _nbytes((HIST_ROWS + tm, d), BF16)
    else:
        assert tm % seq_len == 0 and seq_len >= CONV_WIDTH - 1
        in_specs = [pl.BlockSpec((tm, d), lambda i: (i, 0)),
                    pl.BlockSpec((tm, fh), lambda i: (i, half)),
                    pl.BlockSpec((tm, fh), lambda i: (i, half))] + w_specs
        args = (h2, zs1, zs2, wcat_bf, cw, cb)
        zt_shape, zt_spec = (m, fh), pl.BlockSpec((tm, fh), lambda i: (i, 0))
        block_bytes = _nbytes((tm, d), BF16) + _nbytes((tm, fh), BF16) + 3 * _nbytes((tm, fh), F32)
    return pl.pallas_call(
        functools.partial(_ffn_up_kernel, tm=tm, seq_len=seq_len, streaming=streaming),
        grid=(m // tm,),
        in_specs=in_specs,
        out_specs=(pl.BlockSpec((tm, fh), lambda i: (i, 0)), zt_spec),
        out_shape=(jax.ShapeDtypeStruct((m, fh), BF16), jax.ShapeDtypeStruct(zt_shape, F32)),
        scratch_shapes=scratch,
        compiler_params=pltpu.CompilerParams(
            dimension_semantics=("arbitrary",),
            vmem_limit_bytes=_vmem_limit(block_bytes, scratch_bytes)),
        name="conv_glu_up",
    )(*args)


def _ffn_down(x1, act0, act1, wo_bf, *, tm):
    m, d = x1.shape
    fh = act0.shape[1]
    resident = pl.Buffered(1)
    block_bytes = 2 * _nbytes((tm, d), F32) + 2 * _nbytes((tm, fh), BF16)
    return pl.pallas_call(
        _ffn_down_kernel,
        grid=(m // tm,),
        in_specs=[
            pl.BlockSpec((tm, d), lambda i: (i, 0)),
            pl.BlockSpec((tm, fh), lambda i: (i, 0)),
            pl.BlockSpec((tm, fh), lambda i: (i, 0)),
            pl.BlockSpec((fh, d), lambda i: (0, 0), pipeline_mode=resident),
            pl.BlockSpec((fh, d), lambda i: (1, 0), pipeline_mode=resident),
        ],
        out_specs=pl.BlockSpec((tm, d), lambda i: (i, 0)),
        out_shape=jax.ShapeDtypeStruct((m, d), F32),
        compiler_params=pltpu.CompilerParams(
            dimension_semantics=("arbitrary",),
            vmem_limit_bytes=_vmem_limit(block_bytes, 2 * _nbytes((fh, d), BF16))),
        name="conv_glu_down",
    )(x1, act0, act1, wo_bf, wo_bf)


def _ffn(x1, h2, wcat_bf, cw, cb, wo_bf, *, tm, seq_len, zs1=None, zs2=None):
    ups = [_ffn_up(h2, wcat_bf, cw, cb, half=half, tm=tm, seq_len=seq_len, zs1=zs1, zs2=zs2)
           for half in range(2)]
    y = _ffn_down(x1, ups[0][0], ups[1][0], wo_bf, tm=tm)
    return y, jnp.concatenate([ups[0][1], ups[1][1]], axis=1)


def _prep_weights(attn_norm_g, w_in, q_norm_g, k_norm_g, lambda_q1, lambda_k1, lambda_q2, lambda_k2,
                  subln_g, w_pool, pool_scale, w_up_attn, w_up_pool, w_out,
                  ffn_norm_g, w_ffn_in, conv_w, conv_b, w_ffn_out, *, n_heads):
    d_ff = conv_w.shape[1]
    fp = -(-d_ff // FF_TILE) * FF_TILE
    pad = fp - d_ff
    lam_vec = jnp.pad(jnp.stack([lambda_q1, lambda_k1, lambda_q2, lambda_k2]).astype(F32),
                      ((0, 0), (0, LANES - lambda_q1.shape[0])))
    group = np.arange(MXU_TILE) // QK_HALF
    gd = jnp.asarray((group[:, None] == group[None, :]).astype(np.float32) / QK_HALF, BF16)
    score_bound = (QK_HALF ** 0.5 * BOUND_MARGIN * jnp.max(jnp.abs(q_norm_g), axis=1)
                   * jnp.max(jnp.abs(k_norm_g), axis=1)).astype(F32)
    use_bound = (jnp.max(score_bound) <= MAX_FIXED_SHIFT).astype(jnp.int32).reshape(1)
    w_ffn_cat = _interleave_ffn_in(w_ffn_in, d_ff=d_ff, fp=fp, rows=CAST_ROWS_WIDE)
    return dict(
        score_bound=score_bound,
        use_bound=use_bound,
        attn_g=attn_norm_g.reshape(1, -1),
        w_in=_to_bf16(w_in, rows=CAST_ROWS_WIDE),
        qg=(jnp.tile(q_norm_g.reshape(-1), n_heads) * (QK_HALF ** -0.5)).reshape(1, -1),
        kg=jnp.tile(k_norm_g.reshape(-1), n_heads).reshape(1, -1),
        gd=gd,
        lam_vec=lam_vec,
        sg=subln_g.reshape(1, -1),
        w_pool=w_pool.astype(BF16),
        pool_scale=pool_scale.reshape(1, -1),
        w_up_attn=_to_bf16(w_up_attn, rows=CAST_ROWS),
        w_up_pool=_to_bf16(w_up_pool, rows=CAST_ROWS),
        w_out=_to_bf16(w_out, rows=CAST_ROWS),
        ffn_g=ffn_norm_g.reshape(1, -1),
        w_ffn_cat=w_ffn_cat,
        conv_w=jnp.pad(conv_w, ((0, 0), (0, pad))),
        conv_b=jnp.pad(conv_b, (0, pad)).reshape(1, -1),
        w_ffn_out=_pad_rows_bf16(w_ffn_out, fp=fp, rows=LANES),
        d_ff=d_ff,
    )


def _layer(x, start, caches, layer, pool_hist, conv_hist, lam_init, w, *, n_heads, tm, tq):
    n_seq, t_len, d = x.shape
    m = n_seq * t_len
    d_ff = w["d_ff"]
    x2 = x.reshape(m, d)

    q, k_f32, kb, v_f32, vb, u, sig = _in_proj(
        x2, w["attn_g"], w["w_in"], w["qg"], w["kg"], w["gd"], tm=min(tm, INPROJ_ROWS), n_heads=n_heads)

    if caches is None:
        o = _attn_prompt(q, kb, vb, w["lam_vec"], w["sg"], w["score_bound"], w["use_bound"],
                         tq=tq, tk=tq, lam_init=lam_init)
    else:
        o = _attn_sample(q, kb, vb, caches[0], caches[1], w["lam_vec"], w["sg"],
                         layer=layer, t_new=t_len, lam_init=lam_init)

    hist = jnp.pad(pool_hist, ((0, 0), (HIST_ROWS - POOL_HIST, 0), (0, 0))).reshape(n_seq * HIST_ROWS, -1)
    pool_tm = min(tm, t_len)
    pb = _pool(u, hist, w["w_pool"], w["pool_scale"], n_seq=n_seq, seq_len=t_len, tm=pool_tm, start=start)

    x1, h2 = _merge(o, pb, sig, x2, w["w_up_attn"], w["w_up_pool"], w["w_out"], w["ffn_g"], tm=tm)

    ffn_tm = min(tm, FFN_ROWS)
    if conv_hist is None:
        y, z_tail = _ffn(x1, h2, w["w_ffn_cat"], w["conv_w"], w["conv_b"], w["w_ffn_out"],
                         tm=ffn_tm, seq_len=t_len)
        conv_state = z_tail[-(CONV_WIDTH - 1):, :d_ff][None]
    else:
        fpad = w["conv_b"].shape[1] - d_ff
        zs1 = jnp.pad(conv_hist[:, 1:2], ((0, 0), (0, t_len - 1), (0, fpad))).reshape(m, -1)
        zs2 = jnp.pad(conv_hist, ((0, 0), (0, t_len - (CONV_WIDTH - 1)), (0, fpad))).reshape(m, -1)
        y, z_all = _ffn(x1, h2, w["w_ffn_cat"], w["conv_w"], w["conv_b"], w["w_ffn_out"],
                        tm=ffn_tm, seq_len=t_len, zs1=zs1, zs2=zs2)
        conv_state = z_all.reshape(n_seq, t_len, -1)[:, -(CONV_WIDTH - 1):, :d_ff]

    u3 = u.reshape(n_seq, t_len, -1)
    pool_state = jnp.concatenate([pool_hist, u3], axis=1)[:, -POOL_HIST:] if t_len < POOL_HIST else u3[:, -POOL_HIST:]
    return (y.reshape(n_seq, t_len, d),
            k_f32.reshape(n_seq, t_len, n_heads, HEAD_W),
            v_f32.reshape(n_seq, t_len, n_heads, HEAD_W),
            pool_state, conv_state)


def kernel(x_prompt, x_sample, cache_k, cache_v, state_pool, state_conv, attn_norm_g, w_in, q_norm_g, k_norm_g, lambda_q1, lambda_k1, lambda_q2, lambda_k2, subln_g, w_pool, pool_scale, w_up_attn, w_up_pool, w_out, ffn_norm_g, w_ffn_in, conv_w, conv_b, w_ffn_out):
    depth = w_in.shape[0]
    n_heads = cache_k.shape[3]
    past = cache_k.shape[2]
    y_p, y_s = x_prompt, x_sample
    outs = [[] for _ in range(8)]
    for l in range(depth):
        lam_init = 0.8 - 0.6 * math.exp(-0.3 * l)
        w = _prep_weights(attn_norm_g[l], w_in[l], q_norm_g[l], k_norm_g[l], lambda_q1[l], lambda_k1[l],
                          lambda_q2[l], lambda_k2[l], subln_g[l], w_pool[l], pool_scale[l], w_up_attn[l],
                          w_up_pool[l], w_out[l], ffn_norm_g[l], w_ffn_in[l], conv_w[l], conv_b[l],
                          w_ffn_out[l], n_heads=n_heads)
        pool0 = jnp.zeros((x_prompt.shape[0], POOL_HIST, state_pool.shape[-1]), F32)
        y_p, kp, vp, pp, cp = _layer(y_p, 0, None, l, pool0, None, lam_init, w,
                                     n_heads=n_heads, tm=512, tq=512)
        y_s, ks, vs, ps, cs = _layer(y_s, past, (cache_k, cache_v), l, state_pool[l], state_conv[l],
                                     lam_init, w, n_heads=n_heads,
                                     tm=x_sample.shape[0] * x_sample.shape[1], tq=None)
        for lst, val in zip(outs, (kp, vp, pp, cp, ks, vs, ps, cs)):
            lst.append(val)
    return (y_p, y_s) + tuple(jnp.stack(lst) for lst in outs)
```

```python
import functools
import math

import numpy as np
import jax
import jax.numpy as jnp
from jax import lax
from jax.experimental import pallas as pl
from jax.experimental.pallas import tpu as pltpu

F32 = jnp.float32
BF16 = jnp.bfloat16

EPS = 1e-6
NEG_INF = -1e30
CHUNK = 64
POOL_WINDOWS = (2, 4, 8, 16)
POOL_HIST = max(POOL_WINDOWS) - 1
CONV_WIDTH = 3

LANES = 128
HIST_ROWS = 16
MXU_TILE = 256
VMEM_LIMIT_CAP = 60 * 1024 * 1024
VMEM_TEMP_ALLOWANCE = 12 * 1024 * 1024

HEAD_W = 128
QK_HALF = 64
FF_TILE = 512
SAMPLE_POS_CHUNK = 512
INPROJ_ROWS = 256
FFN_ROWS = 512
CAST_ROWS = 512
CAST_ROWS_WIDE = 128
MAX_FIXED_SHIFT = 40.0
BOUND_MARGIN = 1.02


def _vmem_limit(block_bytes, scratch_bytes):
    est = 2 * block_bytes + scratch_bytes + VMEM_TEMP_ALLOWANCE
    return int(min(est, VMEM_LIMIT_CAP))


def _nbytes(shape, dtype):
    return int(np.prod(shape)) * jnp.dtype(dtype).itemsize


def _rms_rows(x, g):
    ms = jnp.mean(x * x, axis=-1, keepdims=True)
    return x * lax.rsqrt(ms + EPS) * g


def _dot(a, b):
    return jnp.dot(a, b, preferred_element_type=F32)


def _dot_nt(a, b):
    return lax.dot_general(a, b, (((1,), (1,)), ((), ())), preferred_element_type=F32)


def _cast_kernel(x_ref, o_ref):
    o_ref[...] = x_ref[...].astype(BF16)


def _to_bf16(w, *, rows):
    r, c = w.shape
    return pl.pallas_call(
        _cast_kernel,
        grid=(r // rows,),
        in_specs=[pl.BlockSpec((rows, c), lambda i: (i, 0))],
        out_specs=pl.BlockSpec((rows, c), lambda i: (i, 0)),
        out_shape=jax.ShapeDtypeStruct((r, c), BF16),
        compiler_params=pltpu.CompilerParams(
            dimension_semantics=("arbitrary",),
            vmem_limit_bytes=_vmem_limit(_nbytes((rows, c), F32) + _nbytes((rows, c), BF16), 0)),
        name="cast_bf16",
    )(w)


def _interleave_cast_kernel(x_ref, o_ref, *, d_ff):
    rows = x_ref.shape[0]
    for n in range(o_ref.shape[1] // (2 * MXU_TILE)):
        lo = n * MXU_TILE
        width = max(0, min(MXU_TILE, d_ff - lo))
        for part, src in ((0, lo), (1, d_ff + lo)):
            dst = (2 * n + part) * MXU_TILE
            if width:
                o_ref[:, dst:dst + width] = x_ref[:, src:src + width].astype(BF16)
            if width < MXU_TILE:
                o_ref[:, dst + width:dst + MXU_TILE] = jnp.zeros((rows, MXU_TILE - width), BF16)


def _interleave_ffn_in(w, *, d_ff, fp, rows):
    d = w.shape[0]
    return pl.pallas_call(
        functools.partial(_interleave_cast_kernel, d_ff=d_ff),
        grid=(d // rows,),
        in_specs=[pl.BlockSpec((rows, 2 * d_ff), lambda i: (i, 0))],
        out_specs=pl.BlockSpec((rows, 2 * fp), lambda i: (i, 0)),
        out_shape=jax.ShapeDtypeStruct((d, 2 * fp), BF16),
        compiler_params=pltpu.CompilerParams(
            dimension_semantics=("arbitrary",),
            vmem_limit_bytes=_vmem_limit(_nbytes((rows, 2 * d_ff), F32) + _nbytes((rows, 2 * fp), BF16), 0)),
        name="interleave_cast_ffn_in",
    )(w)


def _pad_cast_kernel(x_ref, o_ref, *, n_in):
    keep = pl.program_id(0) < n_in
    o_ref[...] = jnp.where(keep, x_ref[...], 0.0).astype(BF16)


def _pad_rows_bf16(w, *, fp, rows):
    d_ff, d = w.shape
    n_in = d_ff // rows
    return pl.pallas_call(
        functools.partial(_pad_cast_kernel, n_in=n_in),
        grid=(fp // rows,),
        in_specs=[pl.BlockSpec((rows, d), lambda i: (jnp.minimum(i, n_in - 1), 0))],
        out_specs=pl.BlockSpec((rows, d), lambda i: (i, 0)),
        out_shape=jax.ShapeDtypeStruct((fp, d), BF16),
        compiler_params=pltpu.CompilerParams(
            dimension_semantics=("arbitrary",),
            vmem_limit_bytes=_vmem_limit(_nbytes((rows, d), F32) + _nbytes((rows, d), BF16), 0)),
        name="pad_cast_ffn_out",
    )(w)


def _inproj_kernel(x_ref, g_ref, w_ref, qg_ref, kg_ref, gd_ref,
                   q_ref, kf_ref, kb_ref, vf_ref, vb_ref, u_ref, sig_ref, h_scr,
                   *, n_heads):
    tm = x_ref.shape[0]
    aw = n_heads * HEAD_W
    n_col = w_ref.shape[1] // aw
    h_scr[...] = _rms_rows(x_ref[...], g_ref[...]).astype(BF16)

    def col_tile(j):
        return _dot(h_scr[...], w_ref[:, j * aw:(j + 1) * aw])

    def qk_norm(y, gain_ref):
        heads = []
        for c in range(aw // MXU_TILE):
            cs = slice(c * MXU_TILE, (c + 1) * MXU_TILE)
            yc = y[:, cs]
            ms = _dot((yc * yc).astype(BF16), gd_ref[...])
            yn = yc * lax.rsqrt(ms + EPS) * gain_ref[:, cs]
            heads += [yn[:, :HEAD_W], yn[:, HEAD_W:]]
        return heads

    for j in range(4, n_col):
        sig_ref[:, (j - 4) * aw:(j - 3) * aw] = jax.nn.sigmoid(col_tile(j)).astype(BF16)

    feat = lax.broadcasted_iota(jnp.int32, (tm, HEAD_W), 1)
    for h, yh in enumerate(qk_norm(col_tile(0), qg_ref)):
        q_ref[h, 0] = jnp.where(feat < QK_HALF, yh, 0.0).astype(BF16)
        q_ref[h, 1] = jnp.where(feat >= QK_HALF, yh, 0.0).astype(BF16)

    for h, yh in enumerate(qk_norm(col_tile(1), kg_ref)):
        kf_ref[:, h * HEAD_W:(h + 1) * HEAD_W] = yh
        kb_ref[h] = yh.astype(BF16)

    y = col_tile(2)
    vf_ref[...] = y
    for h in range(n_heads):
        vb_ref[h] = y[:, h * HEAD_W:(h + 1) * HEAD_W].astype(BF16)

    u_ref[...] = col_tile(3)


def _in_proj(x, g, w_bf, qg, kg, gd, *, tm, n_heads):
    m, d = x.shape
    aw = n_heads * HEAD_W
    n_all = w_bf.shape[1]
    gate_w = n_all - 4 * aw
    q_shape, q_spec = (n_heads, 2, m, HEAD_W), pl.BlockSpec((n_heads, 2, tm, HEAD_W), lambda i: (0, 0, i, 0))
    v_shape, v_spec = (n_heads, m, HEAD_W), pl.BlockSpec((n_heads, tm, HEAD_W), lambda i: (0, i, 0))
    out_shape = (
        jax.ShapeDtypeStruct(q_shape, BF16),
        jax.ShapeDtypeStruct((m, aw), F32),
        jax.ShapeDtypeStruct((n_heads, m, HEAD_W), BF16),
        jax.ShapeDtypeStruct((m, aw), F32),
        jax.ShapeDtypeStruct(v_shape, BF16),
        jax.ShapeDtypeStruct((m, aw), F32),
        jax.ShapeDtypeStruct((m, gate_w), BF16),
    )
    resident = pl.Buffered(1)
    in_specs = [
        pl.BlockSpec((tm, d), lambda i: (i, 0)),
        pl.BlockSpec((1, d), lambda i: (0, 0)),
        pl.BlockSpec((d, n_all), lambda i: (0, 0), pipeline_mode=resident),
        pl.BlockSpec((1, aw), lambda i: (0, 0)),
        pl.BlockSpec((1, aw), lambda i: (0, 0)),
        pl.BlockSpec((MXU_TILE, MXU_TILE), lambda i: (0, 0)),
    ]
    out_specs = (
        q_spec,
        pl.BlockSpec((tm, aw), lambda i: (i, 0)),
        pl.BlockSpec((n_heads, tm, HEAD_W), lambda i: (0, i, 0)),
        pl.BlockSpec((tm, aw), lambda i: (i, 0)),
        v_spec,
        pl.BlockSpec((tm, aw), lambda i: (i, 0)),
        pl.BlockSpec((tm, gate_w), lambda i: (i, 0)),
    )
    block_bytes = (_nbytes((tm, d), F32) + 3 * _nbytes((tm, aw), F32) + 4 * _nbytes((tm, aw), BF16)
                   + _nbytes((tm, gate_w), BF16))
    return pl.pallas_call(
        functools.partial(_inproj_kernel, n_heads=n_heads),
        grid=(m // tm,),
        in_specs=in_specs,
        out_specs=out_specs,
        out_shape=out_shape,
        scratch_shapes=[pltpu.VMEM((tm, d), BF16)],
        compiler_params=pltpu.CompilerParams(
            dimension_semantics=("arbitrary",),
            vmem_limit_bytes=_vmem_limit(block_bytes, _nbytes((d, n_all), BF16) + _nbytes((tm, d), BF16))),
        name="in_proj",
    )(x, g, w_bf, qg, kg, gd)


def _lambda_value(lam_ref, lam_init):
    a = lam_ref[...]
    s1 = jnp.sum(a[0:1] * a[1:2], axis=1, keepdims=True)
    s2 = jnp.sum(a[2:3] * a[3:4], axis=1, keepdims=True)
    return jnp.exp(s1) - jnp.exp(s2) + lam_init


def _sub_ln(o, sg, lam_init):
    o = o * lax.rsqrt(jnp.mean(o * o, axis=-1, keepdims=True) + EPS)
    return o * sg * (1.0 - lam_init)


def _attn_kernel(qi_ref, kj_ref, flag_ref, fast_ref, bound_ref, lam_ref, sg_ref, q_ref, k_ref, v_ref, o_ref,
                 m_scr, l_scr, acc_scr, *, tq, tk, n_heads, lam_init):
    t = pl.program_id(0)
    i = qi_ref[t]
    j = kj_ref[t]
    flags = flag_ref[t]
    is_first = (flags & 1) != 0
    is_last = (flags & 2) != 0
    is_masked = (flags & 4) != 0
    fast = fast_ref[0] != 0

    @pl.when(is_first)
    def _():
        m_scr[...] = jnp.full(m_scr.shape, NEG_INF, F32)
        l_scr[...] = jnp.zeros(l_scr.shape, F32)
        acc_scr[...] = jnp.zeros(acc_scr.shape, F32)

    def step(masked, bounded):
        if masked:
            qpos = i * tq + lax.broadcasted_iota(jnp.int32, (tq, tk), 0)
            kpos = j * tk + lax.broadcasted_iota(jnp.int32, (tq, tk), 1)
            mask = (kpos // CHUNK) <= (qpos // CHUNK)
            if bounded:
                shift = [jnp.where(mask, -bound_ref[c], NEG_INF) for c in range(2)]

        def head(h, carry):
            kh = k_ref[h]
            vh = v_ref[h]
            for c in range(2):
                s = _dot_nt(q_ref[h, c], kh)
                if bounded and masked:
                    s = s + shift[c]
                elif bounded:
                    s = s - bound_ref[c]
                elif masked:
                    s = jnp.where(mask, s, NEG_INF)
                if bounded:
                    p = jnp.exp(s)
                    psum = p[:, :LANES]
                    for n in range(1, tk // LANES):
                        psum = psum + p[:, n * LANES:(n + 1) * LANES]
                    l_scr[h, c] += psum
                    acc_scr[h, c] += _dot(p.astype(BF16), vh)
                else:
                    m_prev = m_scr[h, c]
                    m_next = jnp.maximum(m_prev, jnp.max(s, axis=1, keepdims=True))
                    p = jnp.exp(s - jnp.tile(m_next, (1, tk // LANES)))
                    alpha = jnp.exp(m_prev - m_next)
                    l_scr[h, c] = alpha * l_scr[h, c] + jnp.sum(p, axis=1, keepdims=True)
                    m_scr[h, c] = m_next
                    acc_scr[h, c] = alpha * acc_scr[h, c] + _dot(p.astype(BF16), vh)
            return carry

        lax.fori_loop(0, n_heads, head, 0, unroll=bounded)

    for masked in (False, True):
        for bounded in (False, True):
            pl.when((is_masked == masked) & (fast == bounded))(functools.partial(step, masked, bounded))

    def finalize(bounded):
        lam = _lambda_value(lam_ref, lam_init)

        def row_sum(h, c):
            l = l_scr[h, c]
            return jnp.sum(l, axis=1, keepdims=True) if bounded else l

        def head(h, carry):
            o = acc_scr[h, 0] / row_sum(h, 0) - lam * (acc_scr[h, 1] / row_sum(h, 1))
            o_ref[h] = _sub_ln(o, sg_ref[...], lam_init).astype(BF16)
            return carry

        lax.fori_loop(0, n_heads, head, 0)

    for bounded in (False, True):
        pl.when(is_last & (fast == bounded))(functools.partial(finalize, bounded))


def _attn_schedule(seq, tq, tk):
    qi, kj, flags = [], [], []
    for i in range(seq // tq):
        q_lo, q_hi = i * tq, (i + 1) * tq - 1
        js = [j for j in range(seq // tk) if (j * tk) // CHUNK <= q_hi // CHUNK]
        for n, j in enumerate(js):
            masked = ((j + 1) * tk - 1) // CHUNK > q_lo // CHUNK
            qi.append(i)
            kj.append(j)
            flags.append((1 if n == 0 else 0) | (2 if n == len(js) - 1 else 0) | (4 if masked else 0))
    return (np.asarray(qi, np.int32), np.asarray(kj, np.int32), np.asarray(flags, np.int32))


def _attn_prompt(q, kb, vb, lam_vec, sg, score_bound, use_bound, *, tq, tk, lam_init):
    n_heads, _, seq, _ = q.shape
    qi, kj, flags = _attn_schedule(seq, tq, tk)
    stat = (n_heads, 2, tq, LANES)
    grid_spec = pltpu.PrefetchScalarGridSpec(
        num_scalar_prefetch=5,
        grid=(len(qi),),
        in_specs=[
            pl.BlockSpec((4, LANES), lambda t, qi, kj, *_: (0, 0)),
            pl.BlockSpec((1, HEAD_W), lambda t, qi, kj, *_: (0, 0)),
            pl.BlockSpec((n_heads, 2, tq, HEAD_W), lambda t, qi, kj, *_: (0, 0, qi[t], 0)),
            pl.BlockSpec((n_heads, tk, HEAD_W), lambda t, qi, kj, *_: (0, kj[t], 0)),
            pl.BlockSpec((n_heads, tk, HEAD_W), lambda t, qi, kj, *_: (0, kj[t], 0)),
        ],
        out_specs=pl.BlockSpec((n_heads, tq, HEAD_W), lambda t, qi, kj, *_: (0, qi[t], 0)),
        scratch_shapes=[pltpu.VMEM(stat, F32), pltpu.VMEM(stat, F32), pltpu.VMEM(stat, F32)],
    )
    block_bytes = (_nbytes((n_heads, 2, tq, HEAD_W), BF16) + 2 * _nbytes((n_heads, tk, HEAD_W), BF16)
                   + _nbytes((n_heads, tq, HEAD_W), BF16))
    return pl.pallas_call(
        functools.partial(_attn_kernel, tq=tq, tk=tk, n_heads=n_heads, lam_init=lam_init),
        grid_spec=grid_spec,
        out_shape=jax.ShapeDtypeStruct((n_heads, seq, HEAD_W), BF16),
        compiler_params=pltpu.CompilerParams(
            dimension_semantics=("arbitrary",),
            vmem_limit_bytes=_vmem_limit(block_bytes, 3 * _nbytes(stat, F32))),
        name="attn_prompt",
    )(jnp.asarray(qi), jnp.asarray(kj), jnp.asarray(flags), use_bound, score_bound, lam_vec, sg, q, kb, vb)


def _attn_dec_kernel(lam_ref, sg_ref, bias_c_ref, bias_n_ref, q_ref, kn_ref, vn_ref, ck_ref, cv_ref, o_ref,
                     *, pos_chunk, t_new, n_heads, lam_init):
    lam = _lambda_value(lam_ref, lam_init)
    past = ck_ref.shape[0]
    rows = n_heads * 2 * t_new
    q = q_ref[...].reshape(rows, HEAD_W)

    def fold(carry, k_flat, v_flat, bias):
        m, l, acc = carry
        s = _dot_nt(q, k_flat) + bias
        m_new = jnp.maximum(m, jnp.max(s, axis=1, keepdims=True))
        alpha = jnp.exp(m - m_new)
        p = jnp.exp(s - m_new)
        return (m_new, alpha * l + jnp.sum(p, axis=1, keepdims=True), alpha * acc + _dot(p.astype(BF16), v_flat))

    carry = (jnp.full((rows, 1), NEG_INF, F32), jnp.zeros((rows, 1), F32), jnp.zeros((rows, HEAD_W), F32))
    for c0 in range(0, past, pos_chunk):
        k_flat = ck_ref[c0:c0 + pos_chunk].reshape(pos_chunk * n_heads, HEAD_W).astype(BF16)
        v_flat = cv_ref[c0:c0 + pos_chunk].reshape(pos_chunk * n_heads, HEAD_W).astype(BF16)
        carry = fold(carry, k_flat, v_flat, bias_c_ref[...])
    carry = fold(carry, kn_ref[...].reshape(n_heads * t_new, HEAD_W), vn_ref[...].reshape(n_heads * t_new, HEAD_W),
                 bias_n_ref[...])
    _, l, acc = carry
    o_all = acc / l
    for h in range(n_heads):
        base = h * 2 * t_new
        o = o_all[base:base + t_new] - lam * o_all[base + t_new:base + 2 * t_new]
        o_ref[h] = _sub_ln(o, sg_ref[...], lam_init).astype(BF16)


def _sample_biases(n_heads, t_new, past, pos_chunk):
    rows = n_heads * 2 * t_new
    row_head = (np.arange(rows) // (2 * t_new))[:, None]
    q_chunk = ((past + np.arange(rows) % t_new) // CHUNK)[:, None]
    assert (past - 1) // CHUNK <= past // CHUNK
    col = np.arange(pos_chunk * n_heads)[None, :]
    bias_c = np.where(col % n_heads == row_head, 0.0, NEG_INF).astype(np.float32)
    col = np.arange(n_heads * t_new)[None, :]
    visible = (col // t_new == row_head) & ((past + col % t_new) // CHUNK <= q_chunk)
    bias_n = np.where(visible, 0.0, NEG_INF).astype(np.float32)
    return jnp.asarray(bias_c), jnp.asarray(bias_n)


def _attn_sample(q, kb, vb, cache_k, cache_v, lam_vec, sg, *, layer, t_new, lam_init):
    n_heads = q.shape[0]
    _, n_seq, past, _, _ = cache_k.shape
    aw = n_heads * HEAD_W
    pos_chunk = min(past, SAMPLE_POS_CHUNK)
    assert past % pos_chunk == 0
    bias_c, bias_n = _sample_biases(n_heads, t_new, past, pos_chunk)
    cache_spec = pl.BlockSpec((None, None, past, n_heads, HEAD_W), lambda b: (layer, b, 0, 0, 0))
    block_bytes = 2 * _nbytes((past, aw), F32) + 5 * _nbytes((n_heads, t_new, HEAD_W), BF16)
    return pl.pallas_call(
        functools.partial(_attn_dec_kernel, pos_chunk=pos_chunk, t_new=t_new, n_heads=n_heads,
                          lam_init=lam_init),
        grid=(n_seq,),
        in_specs=[
            pl.BlockSpec((4, LANES), lambda b: (0, 0)),
            pl.BlockSpec((1, HEAD_W), lambda b: (0, 0)),
            pl.BlockSpec(bias_c.shape, lambda b: (0, 0), pipeline_mode=pl.Buffered(1)),
            pl.BlockSpec(bias_n.shape, lambda b: (0, 0), pipeline_mode=pl.Buffered(1)),
            pl.BlockSpec((n_heads, 2, t_new, HEAD_W), lambda b: (0, 0, b, 0)),
            pl.BlockSpec((n_heads, t_new, HEAD_W), lambda b: (0, b, 0)),
            pl.BlockSpec((n_heads, t_new, HEAD_W), lambda b: (0, b, 0)),
            cache_spec,
            cache_spec,
        ],
        out_specs=pl.BlockSpec((n_heads, t_new, HEAD_W), lambda b: (0, b, 0)),
        out_shape=jax.ShapeDtypeStruct((n_heads, n_seq * t_new, HEAD_W), BF16),
        compiler_params=pltpu.CompilerParams(
            dimension_semantics=("arbitrary",),
            vmem_limit_bytes=_vmem_limit(block_bytes, _nbytes(bias_c.shape, F32) + _nbytes(bias_n.shape, F32))),
        name="attn_sample",
    )(lam_vec, sg, bias_c, bias_n, q, kb, vb, cache_k, cache_v)


def _pool_kernel(u_ref, hist_ref, wp_ref, ps_ref, pb_ref, ext_scr, *, tm, start):
    ti = pl.program_id(1)

    @pl.when(ti == 0)
    def _():
        ext_scr[0:HIST_ROWS] = hist_ref[...]

    @pl.when(ti > 0)
    def _():
        ext_scr[0:HIST_ROWS] = ext_scr[tm:tm + HIST_ROWS]

    ext_scr[HIST_ROWS:HIST_ROWS + tm] = u_ref[...]
    pos = start + ti * tm + lax.broadcasted_iota(jnp.int32, (tm, 1), 0)
    gw = wp_ref.shape[1]
    for g, w in enumerate(POOL_WINDOWS):
        cs = slice(g * gw, (g + 1) * gw)
        cur = ext_scr[HIST_ROWS:HIST_ROWS + tm, cs]
        win = cur
        for back in range(1, w):
            win = win + ext_scr[HIST_ROWS - back:HIST_ROWS - back + tm, cs]
        cnt = jnp.minimum(w, pos + 1).astype(F32)
        p = win / cnt - cur
        y = _dot(p.astype(BF16), wp_ref[g])
        pb_ref[:, cs] = (y * ps_ref[:, cs]).astype(BF16)


def _pool(u, hist, wp_bf, ps, *, n_seq, seq_len, tm, start):
    m, pw = u.shape
    tiles = seq_len // tm
    n_groups, gw, _ = wp_bf.shape
    block_bytes = (_nbytes((tm, pw), F32) + _nbytes((HIST_ROWS, pw), F32) + _nbytes(wp_bf.shape, BF16)
                   + _nbytes((tm, pw), BF16))
    return pl.pallas_call(
        functools.partial(_pool_kernel, tm=tm, start=start),
        grid=(n_seq, tiles),
        in_specs=[
            pl.BlockSpec((tm, pw), lambda b, t: (b * tiles + t, 0)),
            pl.BlockSpec((HIST_ROWS, pw), lambda b, t: (b, 0)),
            pl.BlockSpec((n_groups, gw, gw), lambda b, t: (0, 0, 0)),
            pl.BlockSpec((1, pw), lambda b, t: (0, 0)),
        ],
        out_specs=pl.BlockSpec((tm, pw), lambda b, t: (b * tiles + t, 0)),
        out_shape=jax.ShapeDtypeStruct((m, pw), BF16),
        scratch_shapes=[pltpu.VMEM((HIST_ROWS + tm, pw), F32)],
        compiler_params=pltpu.CompilerParams(
            dimension_semantics=("arbitrary", "arbitrary"),
            vmem_limit_bytes=_vmem_limit(block_bytes, _nbytes((HIST_ROWS + tm, pw), F32))),
        name="pool_mix",
    )(u, hist, wp_bf, ps)


def _merge_kernel(o_ref, pb_ref, sig_ref, x_ref, wua_ref, wup_ref, wout_ref, g2_ref, x1_ref, h2_ref, m_scr,
                  *, n_heads):
    d = x_ref.shape[1]
    o = jnp.concatenate([o_ref[h] for h in range(n_heads)], axis=1)
    pb = pb_ref[...]
    for n in range(d // FF_TILE):
        cs = slice(n * FF_TILE, (n + 1) * FF_TILE)
        gs = slice(d + n * FF_TILE, d + (n + 1) * FF_TILE)
        a = _dot(o, wua_ref[:, cs])
        b = _dot(pb, wup_ref[:, cs])
        m_scr[:, cs] = (sig_ref[:, cs].astype(F32) * a + sig_ref[:, gs].astype(F32) * b).astype(BF16)
    for n in range(d // FF_TILE):
        cs = slice(n * FF_TILE, (n + 1) * FF_TILE)
        x1_ref[:, cs] = x_ref[:, cs] + _dot(m_scr[...], wout_ref[:, cs])
    h2_ref[...] = _rms_rows(x1_ref[...], g2_ref[...]).astype(BF16)


def _merge(o, pb, sig, x, wua_bf, wup_bf, wout_bf, ffn_g, *, tm):
    n_heads = o.shape[0]
    m, d = x.shape
    aw = n_heads * HEAD_W
    block_bytes = (2 * _nbytes((tm, aw), BF16) + _nbytes((tm, 2 * d), BF16) + 2 * _nbytes((tm, d), F32)
                   + _nbytes((tm, d), BF16))
    weight_bytes = 2 * _nbytes((aw, d), BF16) + _nbytes((d, d), BF16)
    resident = pl.Buffered(1)
    return pl.pallas_call(
        functools.partial(_merge_kernel, n_heads=n_heads),
        grid=(m // tm,),
        in_specs=[
            pl.BlockSpec((n_heads, tm, HEAD_W), lambda i: (0, i, 0)),
            pl.BlockSpec((tm, aw), lambda i: (i, 0)),
            pl.BlockSpec((tm, 2 * d), lambda i: (i, 0)),
            pl.BlockSpec((tm, d), lambda i: (i, 0)),
            pl.BlockSpec((aw, d), lambda i: (0, 0), pipeline_mode=resident),
            pl.BlockSpec((aw, d), lambda i: (0, 0), pipeline_mode=resident),
            pl.BlockSpec((d, d), lambda i: (0, 0), pipeline_mode=resident),
            pl.BlockSpec((1, d), lambda i: (0, 0)),
        ],
        out_specs=(pl.BlockSpec((tm, d), lambda i: (i, 0)), pl.BlockSpec((tm, d), lambda i: (i, 0))),
        out_shape=(jax.ShapeDtypeStruct((m, d), F32), jax.ShapeDtypeStruct((m, d), BF16)),
        scratch_shapes=[pltpu.VMEM((tm, d), BF16)],
        compiler_params=pltpu.CompilerParams(
            dimension_semantics=("arbitrary",),
            vmem_limit_bytes=_vmem_limit(block_bytes, weight_bytes + _nbytes((tm, d), BF16))),
        name="merge_out_proj",
    )(o, pb, sig, x, wua_bf, wup_bf, wout_bf, ffn_g)


def _ffn_up_kernel(*refs, tm, seq_len, streaming):
    if streaming:
        h_ref, hp_ref, w_ref, cw_ref, cb_ref, act_ref, zt_ref, hext_scr = refs
        hext_scr[0:HIST_ROWS] = hp_ref[...]
        hext_scr[HIST_ROWS:HIST_ROWS + tm] = h_ref[...]
        lhs_ref = hext_scr
        not_first = pl.program_id(0) > 0
    else:
        h_ref, zs1_ref, zs2_ref, w_ref, cw_ref, cb_ref, act_ref, zt_ref = refs
        lhs_ref = h_ref
        r = lax.broadcasted_iota(jnp.int32, (tm, 1), 0) % seq_len
    for n in range(w_ref.shape[1] // (2 * MXU_TILE)):
        cs = slice(n * MXU_TILE, (n + 1) * MXU_TILE)
        zv = _dot(lhs_ref[...], w_ref[:, 2 * n * MXU_TILE:(2 * n + 2) * MXU_TILE])
        if streaming:
            row = lax.broadcasted_iota(jnp.int32, (HIST_ROWS + tm, 1), 0)
            zx = jnp.where((row >= HIST_ROWS) | not_first, zv[:, :MXU_TILE], 0.0)
            z = zx[HIST_ROWS:]
            val = zv[HIST_ROWS:, MXU_TILE:]
            z1 = pltpu.roll(zx, 1, axis=0)[HIST_ROWS:]
            z2 = pltpu.roll(zx, 2, axis=0)[HIST_ROWS:]
            zt_ref[:, cs] = z[tm - 8:tm]
        else:
            z = zv[:, :MXU_TILE]
            val = zv[:, MXU_TILE:]
            z1 = jnp.where(r >= 1, pltpu.roll(z, 1, axis=0), 0.0) + zs1_ref[:, cs]
            z2 = jnp.where(r >= 2, pltpu.roll(z, 2, axis=0), 0.0) + zs2_ref[:, cs]
            zt_ref[:, cs] = z
        zc = cb_ref[:, cs] + z2 * cw_ref[0:1, cs] + z1 * cw_ref[1:2, cs] + z * cw_ref[2:3, cs]
        act_ref[:, cs] = (jax.nn.silu(zc) * val).astype(BF16)


def _ffn_down_kernel(x_ref, a0_ref, a1_ref, w0_ref, w1_ref, y_ref):
    y_ref[...] = x_ref[...] + _dot(a0_ref[...], w0_ref[...]) + _dot(a1_ref[...], w1_ref[...])


def _ffn_up(h2, wcat_bf, cw, cb, *, half, tm, seq_len, zs1=None, zs2=None):
    m, d = h2.shape
    fh = wcat_bf.shape[1] // 4
    streaming = zs1 is None
    w_specs = [
        pl.BlockSpec((d, 2 * fh), lambda i: (0, half), pipeline_mode=pl.Buffered(1)),
        pl.BlockSpec((CONV_WIDTH, fh), lambda i: (0, half)),
        pl.BlockSpec((1, fh), lambda i: (0, half)),
    ]
    scratch = []
    scratch_bytes = _nbytes((d, 2 * fh), BF16)
    if streaming:
        assert seq_len == m and tm % HIST_ROWS == 0
        per_tile = tm // HIST_ROWS
        in_specs = [pl.BlockSpec((tm, d), lambda i: (i, 0)),
                    pl.BlockSpec((HIST_ROWS, d), lambda i: (jnp.maximum(i * per_tile - 1, 0), 0))] + w_specs
        args = (h2, h2, wcat_bf, cw, cb)
        zt_shape, zt_spec = (m // tm * 8, fh), pl.BlockSpec((8, fh), lambda i: (i, 0))
        block_bytes = _nbytes((tm, d), BF16) + _nbytes((tm, fh), BF16)
        scratch.append(pltpu.VMEM((HIST_ROWS + tm, d), BF16))
        scratch_bytes += _nbytes((HIST_ROWS + tm, d), BF16)
    else:
        assert tm % seq_len == 0 and seq_len >= CONV_WIDTH - 1
        in_specs = [pl.BlockSpec((tm, d), lambda i: (i, 0)),
                    pl.BlockSpec((tm, fh), lambda i: (i, half)),
                    pl.BlockSpec((tm, fh), lambda i: (i, half))] + w_specs
        args = (h2, zs1, zs2, wcat_bf, cw, cb)
        zt_shape, zt_spec = (m, fh), pl.BlockSpec((tm, fh), lambda i: (i, 0))
        block_bytes = _nbytes((tm, d), BF16) + _nbytes((tm, fh), BF16) + 3 * _nbytes((tm, fh), F32)
    return pl.pallas_call(
        functools.partial(_ffn_up_kernel, tm=tm, seq_len=seq_len, streaming=streaming),
        grid=(m // tm,),
        in_specs=in_specs,
        out_specs=(pl.BlockSpec((tm, fh), lambda i: (i, 0)), zt_spec),
        out_shape=(jax.ShapeDtypeStruct((m, fh), BF16), jax.ShapeDtypeStruct(zt_shape, F32)),
        scratch_shapes=scratch,
        compiler_params=pltpu.CompilerParams(
            dimension_semantics=("arbitrary",),
            vmem_limit_bytes=_vmem_limit(block_bytes, scratch_bytes)),
        name="conv_glu_up",
    )(*args)


def _ffn_down(x1, act0, act1, wo_bf, *, tm):
    m, d = x1.shape
    fh = act0.shape[1]
    resident = pl.Buffered(1)
    block_bytes = 2 * _nbytes((tm, d), F32) + 2 * _nbytes((tm, fh), BF16)
    return pl.pallas_call(
        _ffn_down_kernel,
        grid=(m // tm,),
        in_specs=[
            pl.BlockSpec((tm, d), lambda i: (i, 0)),
            pl.BlockSpec((tm, fh), lambda i: (i, 0)),
            pl.BlockSpec((tm, fh), lambda i: (i, 0)),
            pl.BlockSpec((fh, d), lambda i: (0, 0), pipeline_mode=resident),
            pl.BlockSpec((fh, d), lambda i: (1, 0), pipeline_mode=resident),
        ],
        out_specs=pl.BlockSpec((tm, d), lambda i: (i, 0)),
        out_shape=jax.ShapeDtypeStruct((m, d), F32),
        compiler_params=pltpu.CompilerParams(
            dimension_semantics=("arbitrary",),
            vmem_limit_bytes=_vmem_limit(block_bytes, 2 * _nbytes((fh, d), BF16))),
        name="conv_glu_down",
    )(x1, act0, act1, wo_bf, wo_bf)


def _ffn(x1, h2, wcat_bf, cw, cb, wo_bf, *, tm, seq_len, zs1=None, zs2=None):
    ups = [_ffn_up(h2, wcat_bf, cw, cb, half=half, tm=tm, seq_len=seq_len, zs1=zs1, zs2=zs2)
           for half in range(2)]
    y = _ffn_down(x1, ups[0][0], ups[1][0], wo_bf, tm=tm)
    return y, jnp.concatenate([ups[0][1], ups[1][1]], axis=1)


def _prep_weights(attn_norm_g, w_in, q_norm_g, k_norm_g, lambda_q1, lambda_k1, lambda_q2, lambda_k2,
                  subln_g, w_pool, pool_scale, w_up_attn, w_up_pool, w_out,
                  ffn_norm_g, w_ffn_in, conv_w, conv_b, w_ffn_out, *, n_heads):
    d_ff = conv_w.shape[1]
    fp = -(-d_ff // FF_TILE) * FF_TILE
    pad = fp - d_ff
    lam_vec = jnp.pad(jnp.stack([lambda_q1, lambda_k1, lambda_q2, lambda_k2]).astype(F32),
                      ((0, 0), (0, LANES - lambda_q1.shape[0])))
    group = np.arange(MXU_TILE) // QK_HALF
    gd = jnp.asarray((group[:, None] == group[None, :]).astype(np.float32) / QK_HALF, BF16)
    score_bound = (QK_HALF ** 0.5 * BOUND_MARGIN * jnp.max(jnp.abs(q_norm_g), axis=1)
                   * jnp.max(jnp.abs(k_norm_g), axis=1)).astype(F32)
    use_bound = (jnp.max(score_bound) <= MAX_FIXED_SHIFT).astype(jnp.int32).reshape(1)
    w_ffn_cat = _interleave_ffn_in(w_ffn_in, d_ff=d_ff, fp=fp, rows=CAST_ROWS_WIDE)
    return dict(
        score_bound=score_bound,
        use_bound=use_bound,
        attn_g=attn_norm_g.reshape(1, -1),
        w_in=_to_bf16(w_in, rows=CAST_ROWS_WIDE),
        qg=(jnp.tile(q_norm_g.reshape(-1), n_heads) * (QK_HALF ** -0.5)).reshape(1, -1),
        kg=jnp.tile(k_norm_g.reshape(-1), n_heads).reshape(1, -1),
        gd=gd,
        lam_vec=lam_vec,
        sg=subln_g.reshape(1, -1),
        w_pool=w_pool.astype(BF16),
        pool_scale=pool_scale.reshape(1, -1),
        w_up_attn=_to_bf16(w_up_attn, rows=CAST_ROWS),
        w_up_pool=_to_bf16(w_up_pool, rows=CAST_ROWS),
        w_out=_to_bf16(w_out, rows=CAST_ROWS),
        ffn_g=ffn_norm_g.reshape(1, -1),
        w_ffn_cat=w_ffn_cat,
        conv_w=jnp.pad(conv_w, ((0, 0), (0, pad))),
        conv_b=jnp.pad(conv_b, (0, pad)).reshape(1, -1),
        w_ffn_out=_pad_rows_bf16(w_ffn_out, fp=fp, rows=LANES),
        d_ff=d_ff,
    )


def _layer(x, start, caches, layer, pool_hist, conv_hist, lam_init, w, *, n_heads, tm, tq):
    n_seq, t_len, d = x.shape
    m = n_seq * t_len
    d_ff = w["d_ff"]
    x2 = x.reshape(m, d)

    q, k_f32, kb, v_f32, vb, u, sig = _in_proj(
        x2, w["attn_g"], w["w_in"], w["qg"], w["kg"], w["gd"], tm=min(tm, INPROJ_ROWS), n_heads=n_heads)

    if caches is None:
        o = _attn_prompt(q, kb, vb, w["lam_vec"], w["sg"], w["score_bound"], w["use_bound"],
                         tq=tq, tk=tq, lam_init=lam_init)
    else:
        o = _attn_sample(q, kb, vb, caches[0], caches[1], w["lam_vec"], w["sg"],
                         layer=layer, t_new=t_len, lam_init=lam_init)

    hist = jnp.pad(pool_hist, ((0, 0), (HIST_ROWS - POOL_HIST, 0), (0, 0))).reshape(n_seq * HIST_ROWS, -1)
    pool_tm = min(tm, t_len)
    pb = _pool(u, hist, w["w_pool"], w["pool_scale"], n_seq=n_seq, seq_len=t_len, tm=pool_tm, start=start)

    x1, h2 = _merge(o, pb, sig, x2, w["w_up_attn"], w["w_up_pool"], w["w_out"], w["ffn_g"], tm=tm)

    ffn_tm = min(tm, FFN_ROWS)
    if conv_hist is None:
        y, z_tail = _ffn(x1, h2, w["w_ffn_cat"], w["conv_w"], w["conv_b"], w["w_ffn_out"],
                         tm=ffn_tm, seq_len=t_len)
        conv_state = z_tail[-(CONV_WIDTH - 1):, :d_ff][None]
    else:
        fpad = w["conv_b"].shape[1] - d_ff
        zs1 = jnp.pad(conv_hist[:, 1:2], ((0, 0), (0, t_len - 1), (0, fpad))).reshape(m, -1)
        zs2 = jnp.pad(conv_hist, ((0, 0), (0, t_len - (CONV_WIDTH - 1)), (0, fpad))).reshape(m, -1)
        y, z_all = _ffn(x1, h2, w["w_ffn_cat"], w["conv_w"], w["conv_b"], w["w_ffn_out"],
                        tm=ffn_tm, seq_len=t_len, zs1=zs1, zs2=zs2)
        conv_state = z_all.reshape(n_seq, t_len, -1)[:, -(CONV_WIDTH - 1):, :d_ff]

    u3 = u.reshape(n_seq, t_len, -1)
    pool_state = jnp.concatenate([pool_hist, u3], axis=1)[:, -POOL_HIST:] if t_len < POOL_HIST else u3[:, -POOL_HIST:]
    return (y.reshape(n_seq, t_len, d),
            k_f32.reshape(n_seq, t_len, n_heads, HEAD_W),
            v_f32.reshape(n_seq, t_len, n_heads, HEAD_W),
            pool_state, conv_state)


def kernel(x_prompt, x_sample, cache_k, cache_v, state_pool, state_conv, attn_norm_g, w_in, q_norm_g, k_norm_g, lambda_q1, lambda_k1, lambda_q2, lambda_k2, subln_g, w_pool, pool_scale, w_up_attn, w_up_pool, w_out, ffn_norm_g, w_ffn_in, conv_w, conv_b, w_ffn_out):
    depth = w_in.shape[0]
    n_heads = cache_k.shape[3]
    past = cache_k.shape[2]
    y_p, y_s = x_prompt, x_sample
    outs = [[] for _ in range(8)]
    for l in range(depth):
        lam_init = 0.8 - 0.6 * math.exp(-0.3 * l)
        w = _prep_weights(attn_norm_g[l], w_in[l], q_norm_g[l], k_norm_g[l], lambda_q1[l], lambda_k1[l],
                          lambda_q2[l], lambda_k2[l], subln_g[l], w_pool[l], pool_scale[l], w_up_attn[l],
                          w_up_pool[l], w_out[l], ffn_norm_g[l], w_ffn_in[l], conv_w[l], conv_b[l],
                          w_ffn_out[l], n_heads=n_heads)
        pool0 = jnp.zeros((x_prompt.shape[0], POOL_HIST, state_pool.shape[-1]), F32)
        y_p, kp, vp, pp, cp = _layer(y_p, 0, None, l, pool0, None, lam_init, w,
                                     n_heads=n_heads, tm=512, tq=512)
        y_s, ks, vs, ps, cs = _layer(y_s, past, (cache_k, cache_v), l, state_pool[l], state_conv[l],
                                     lam_init, w, n_heads=n_heads,
                                     tm=x_sample.shape[0] * x_sample.shape[1], tq=None)
        for lst, val in zip(outs, (kp, vp, pp, cp, ks, vs, ps, cs)):
            lst.append(val)
    return (y_p, y_s) + tuple(jnp.stack(lst) for lst in outs)
```

```python
import functools
import math

import numpy as np
import jax
import jax.numpy as jnp
from jax import lax
from jax.experimental import pallas as pl
from jax.experimental.pallas import tpu as pltpu

F32 = jnp.float32
BF16 = jnp.bfloat16

EPS = 1e-6
NEG_INF = -1e30
CHUNK = 64
POOL_WINDOWS = (2, 4, 8, 16)
POOL_HIST = max(POOL_WINDOWS) - 1
CONV_WIDTH = 3

LANES = 128
HIST_ROWS = 16
MXU_TILE = 256
VMEM_LIMIT_CAP = 60 * 1024 * 1024
VMEM_TEMP_ALLOWANCE = 12 * 1024 * 1024

HEAD_W = 128
QK_HALF = 64
FF_TILE = 512
SAMPLE_POS_CHUNK = 512
INPROJ_ROWS = 256
FFN_ROWS = 512
CAST_ROWS = 512
CAST_ROWS_WIDE = 128
MAX_FIXED_SHIFT = 40.0
BOUND_MARGIN = 1.02


def _vmem_limit(block_bytes, scratch_bytes):
    est = 2 * block_bytes + scratch_bytes + VMEM_TEMP_ALLOWANCE
    return int(min(est, VMEM_LIMIT_CAP))


def _nbytes(shape, dtype):
    return int(np.prod(shape)) * jnp.dtype(dtype).itemsize


def _rms_rows(x, g):
    ms = jnp.mean(x * x, axis=-1, keepdims=True)
    return x * lax.rsqrt(ms + EPS) * g


def _dot(a, b):
    return jnp.dot(a, b, preferred_element_type=F32)


def _dot_nt(a, b):
    return lax.dot_general(a, b, (((1,), (1,)), ((), ())), preferred_element_type=F32)


def _cast_kernel(x_ref, o_ref):
    o_ref[...] = x_ref[...].astype(BF16)


def _to_bf16(w, *, rows):
    r, c = w.shape
    return pl.pallas_call(
        _cast_kernel,
        grid=(r // rows,),
        in_specs=[pl.BlockSpec((rows, c), lambda i: (i, 0))],
        out_specs=pl.BlockSpec((rows, c), lambda i: (i, 0)),
        out_shape=jax.ShapeDtypeStruct((r, c), BF16),
        compiler_params=pltpu.CompilerParams(
            dimension_semantics=("arbitrary",),
            vmem_limit_bytes=_vmem_limit(_nbytes((rows, c), F32) + _nbytes((rows, c), BF16), 0)),
        name="cast_bf16",
    )(w)


def _interleave_cast_kernel(x_ref, o_ref, *, d_ff):
    rows = x_ref.shape[0]
    for n in range(o_ref.shape[1] // (2 * MXU_TILE)):
        lo = n * MXU_TILE
        width = max(0, min(MXU_TILE, d_ff - lo))
        for part, src in ((0, lo), (1, d_ff + lo)):
            dst = (2 * n + part) * MXU_TILE
            if width:
                o_ref[:, dst:dst + width] = x_ref[:, src:src + width].astype(BF16)
            if width < MXU_TILE:
                o_ref[:, dst + width:dst + MXU_TILE] = jnp.zeros((rows, MXU_TILE - width), BF16)


def _interleave_ffn_in(w, *, d_ff, fp, rows):
    d = w.shape[0]
    return pl.pallas_call(
        functools.partial(_interleave_cast_kernel, d_ff=d_ff),
        grid=(d // rows,),
        in_specs=[pl.BlockSpec((rows, 2 * d_ff), lambda i: (i, 0))],
        out_specs=pl.BlockSpec((rows, 2 * fp), lambda i: (i, 0)),
        out_shape=jax.ShapeDtypeStruct((d, 2 * fp), BF16),
        compiler_params=pltpu.CompilerParams(
            dimension_semantics=("arbitrary",),
            vmem_limit_bytes=_vmem_limit(_nbytes((rows, 2 * d_ff), F32) + _nbytes((rows, 2 * fp), BF16), 0)),
        name="interleave_cast_ffn_in",
    )(w)


def _pad_cast_kernel(x_ref, o_ref, *, n_in):
    keep = pl.program_id(0) < n_in
    o_ref[...] = jnp.where(keep, x_ref[...], 0.0).astype(BF16)


def _pad_rows_bf16(w, *, fp, rows):
    d_ff, d = w.shape
    n_in = d_ff // rows
    return pl.pallas_call(
        functools.partial(_pad_cast_kernel, n_in=n_in),
        grid=(fp // rows,),
        in_specs=[pl.BlockSpec((rows, d), lambda i: (jnp.minimum(i, n_in - 1), 0))],
        out_specs=pl.BlockSpec((rows, d), lambda i: (i, 0)),
        out_shape=jax.ShapeDtypeStruct((fp, d), BF16),
        compiler_params=pltpu.CompilerParams(
            dimension_semantics=("arbitrary",),
            vmem_limit_bytes=_vmem_limit(_nbytes((rows, d), F32) + _nbytes((rows, d), BF16), 0)),
        name="pad_cast_ffn_out",
    )(w)


def _inproj_kernel(x_ref, g_ref, w_ref, qg_ref, kg_ref, gd_ref,
                   q_ref, kf_ref, kb_ref, vf_ref, vb_ref, u_ref, sig_ref, h_scr,
                   *, n_heads):
    tm = x_ref.shape[0]
    aw = n_heads * HEAD_W
    n_col = w_ref.shape[1] // aw
    h_scr[...] = _rms_rows(x_ref[...], g_ref[...]).astype(BF16)

    def col_tile(j):
        return _dot(h_scr[...], w_ref[:, j * aw:(j + 1) * aw])

    def qk_norm(y, gain_ref):
        heads = []
        for c in range(aw // MXU_TILE):
            cs = slice(c * MXU_TILE, (c + 1) * MXU_TILE)
            yc = y[:, cs]
            ms = _dot((yc * yc).astype(BF16), gd_ref[...])
            yn = yc * lax.rsqrt(ms + EPS) * gain_ref[:, cs]
            heads += [yn[:, :HEAD_W], yn[:, HEAD_W:]]
        return heads

    for j in range(4, n_col):
        sig_ref[:, (j - 4) * aw:(j - 3) * aw] = jax.nn.sigmoid(col_tile(j)).astype(BF16)

    feat = lax.broadcasted_iota(jnp.int32, (tm, HEAD_W), 1)
    for h, yh in enumerate(qk_norm(col_tile(0), qg_ref)):
        q_ref[h, 0] = jnp.where(feat < QK_HALF, yh, 0.0).astype(BF16)
        q_ref[h, 1] = jnp.where(feat >= QK_HALF, yh, 0.0).astype(BF16)

    for h, yh in enumerate(qk_norm(col_tile(1), kg_ref)):
        kf_ref[:, h * HEAD_W:(h + 1) * HEAD_W] = yh
        kb_ref[h] = yh.astype(BF16)

    y = col_tile(2)
    vf_ref[...] = y
    for h in range(n_heads):
        vb_ref[h] = y[:, h * HEAD_W:(h + 1) * HEAD_W].astype(BF16)

    u_ref[...] = col_tile(3)


def _in_proj(x, g, w_bf, qg, kg, gd, *, tm, n_heads):
    m, d = x.shape
    aw = n_heads * HEAD_W
    n_all = w_bf.shape[1]
    gate_w = n_all - 4 * aw
    q_shape, q_spec = (n_heads, 2, m, HEAD_W), pl.BlockSpec((n_heads, 2, tm, HEAD_W), lambda i: (0, 0, i, 0))
    v_shape, v_spec = (n_heads, m, HEAD_W), pl.BlockSpec((n_heads, tm, HEAD_W), lambda i: (0, i, 0))
    out_shape = (
        jax.ShapeDtypeStruct(q_shape, BF16),
        jax.ShapeDtypeStruct((m, aw), F32),
        jax.ShapeDtypeStruct((n_heads, m, HEAD_W), BF16),
        jax.ShapeDtypeStruct((m, aw), F32),
        jax.ShapeDtypeStruct(v_shape, BF16),
        jax.ShapeDtypeStruct((m, aw), F32),
        jax.ShapeDtypeStruct((m, gate_w), BF16),
    )
    resident = pl.Buffered(1)
    in_specs = [
        pl.BlockSpec((tm, d), lambda i: (i, 0)),
        pl.BlockSpec((1, d), lambda i: (0, 0)),
        pl.BlockSpec((d, n_all), lambda i: (0, 0), pipeline_mode=resident),
        pl.BlockSpec((1, aw), lambda i: (0, 0)),
        pl.BlockSpec((1, aw), lambda i: (0, 0)),
        pl.BlockSpec((MXU_TILE, MXU_TILE), lambda i: (0, 0)),
    ]
    out_specs = (
        q_spec,
        pl.BlockSpec((tm, aw), lambda i: (i, 0)),
        pl.BlockSpec((n_heads, tm, HEAD_W), lambda i: (0, i, 0)),
        pl.BlockSpec((tm, aw), lambda i: (i, 0)),
        v_spec,
        pl.BlockSpec((tm, aw), lambda i: (i, 0)),
        pl.BlockSpec((tm, gate_w), lambda i: (i, 0)),
    )
    block_bytes = (_nbytes((tm, d), F32) + 3 * _nbytes((tm, aw), F32) + 4 * _nbytes((tm, aw), BF16)
                   + _nbytes((tm, gate_w), BF16))
    return pl.pallas_call(
        functools.partial(_inproj_kernel, n_heads=n_heads),
        grid=(m // tm,),
        in_specs=in_specs,
        out_specs=out_specs,
        out_shape=out_shape,
        scratch_shapes=[pltpu.VMEM((tm, d), BF16)],
        compiler_params=pltpu.CompilerParams(
            dimension_semantics=("arbitrary",),
            vmem_limit_bytes=_vmem_limit(block_bytes, _nbytes((d, n_all), BF16) + _nbytes((tm, d), BF16))),
        name="in_proj",
    )(x, g, w_bf, qg, kg, gd)


def _lambda_value(lam_ref, lam_init):
    a = lam_ref[...]
    s1 = jnp.sum(a[0:1] * a[1:2], axis=1, keepdims=True)
    s2 = jnp.sum(a[2:3] * a[3:4], axis=1, keepdims=True)
    return jnp.exp(s1) - jnp.exp(s2) + lam_init


def _sub_ln(o, sg, lam_init):
    o = o * lax.rsqrt(jnp.mean(o * o, axis=-1, keepdims=True) + EPS)
    return o * sg * (1.0 - lam_init)


def _attn_kernel(qi_ref, kj_ref, flag_ref, fast_ref, bound_ref, lam_ref, sg_ref, q_ref, k_ref, v_ref, o_ref,
                 m_scr, l_scr, acc_scr, *, tq, tk, n_heads, lam_init):
    t = pl.program_id(0)
    i = qi_ref[t]
    j = kj_ref[t]
    flags = flag_ref[t]
    is_first = (flags & 1) != 0
    is_last = (flags & 2) != 0
    is_masked = (flags & 4) != 0
    fast = fast_ref[0] != 0

    @pl.when(is_first)
    def _():
        m_scr[...] = jnp.full(m_scr.shape, NEG_INF, F32)
        l_scr[...] = jnp.zeros(l_scr.shape, F32)
        acc_scr[...] = jnp.zeros(acc_scr.shape, F32)

    def step(masked, bounded):
        if masked:
            qpos = i * tq + lax.broadcasted_iota(jnp.int32, (tq, tk), 0)
            kpos = j * tk + lax.broadcasted_iota(jnp.int32, (tq, tk), 1)
            mask = (kpos // CHUNK) <= (qpos // CHUNK)
            if bounded:
                shift = [jnp.where(mask, -bound_ref[c], NEG_INF) for c in range(2)]

        def head(h, carry):
            kh = k_ref[h]
            vh = v_ref[h]
            for c in range(2):
                s = _dot_nt(q_ref[h, c], kh)
                if bounded and masked:
                    s = s + shift[c]
                elif bounded:
                    s = s - bound_ref[c]
                elif masked:
                    s = jnp.where(mask, s, NEG_INF)
                if bounded:
                    p = jnp.exp(s)
                    psum = p[:, :LANES]
                    for n in range(1, tk // LANES):
                        psum = psum + p[:, n * LANES:(n + 1) * LANES]
                    l_scr[h, c] += psum
                    acc_scr[h, c] += _dot(p.astype(BF16), vh)
                else:
                    m_prev = m_scr[h, c]
                    m_next = jnp.maximum(m_prev, jnp.max(s, axis=1, keepdims=True))
                    p = jnp.exp(s - jnp.tile(m_next, (1, tk // LANES)))
                    alpha = jnp.exp(m_prev - m_next)
                    l_scr[h, c] = alpha * l_scr[h, c] + jnp.sum(p, axis=1, keepdims=True)
                    m_scr[h, c] = m_next
                    acc_scr[h, c] = alpha * acc_scr[h, c] + _dot(p.astype(BF16), vh)
            return carry

        lax.fori_loop(0, n_heads, head, 0, unroll=bounded)

    for masked in (False, True):
        for bounded in (False, True):
            pl.when((is_masked == masked) & (fast == bounded))(functools.partial(step, masked, bounded))

    def finalize(bounded):
        lam = _lambda_value(lam_ref, lam_init)

        def row_sum(h, c):
            l = l_scr[h, c]
            return jnp.sum(l, axis=1, keepdims=True) if bounded else l

        def head(h, carry):
            o = acc_scr[h, 0] / row_sum(h, 0) - lam * (acc_scr[h, 1] / row_sum(h, 1))
            o_ref[h] = _sub_ln(o, sg_ref[...], lam_init).astype(BF16)
            return carry

        lax.fori_loop(0, n_heads, head, 0)

    for bounded in (False, True):
        pl.when(is_last & (fast == bounded))(functools.partial(finalize, bounded))


def _attn_schedule(seq, tq, tk):
    qi, kj, flags = [], [], []
    for i in range(seq // tq):
        q_lo, q_hi = i * tq, (i + 1) * tq - 1
        js = [j for j in range(seq // tk) if (j * tk) // CHUNK <= q_hi // CHUNK]
        for n, j in enumerate(js):
            masked = ((j + 1) * tk - 1) // CHUNK > q_lo // CHUNK
            qi.append(i)
            kj.append(j)
            flags.append((1 if n == 0 else 0) | (2 if n == len(js) - 1 else 0) | (4 if masked else 0))
    return (np.asarray(qi, np.int32), np.asarray(kj, np.int32), np.asarray(flags, np.int32))


def _attn_prompt(q, kb, vb, lam_vec, sg, score_bound, use_bound, *, tq, tk, lam_init):
    n_heads, _, seq, _ = q.shape
    qi, kj, flags = _attn_schedule(seq, tq, tk)
    stat = (n_heads, 2, tq, LANES)
    grid_spec = pltpu.PrefetchScalarGridSpec(
        num_scalar_prefetch=5,
        grid=(len(qi),),
        in_specs=[
            pl.BlockSpec((4, LANES), lambda t, qi, kj, *_: (0, 0)),
            pl.BlockSpec((1, HEAD_W), lambda t, qi, kj, *_: (0, 0)),
            pl.BlockSpec((n_heads, 2, tq, HEAD_W), lambda t, qi, kj, *_: (0, 0, qi[t], 0)),
            pl.BlockSpec((n_heads, tk, HEAD_W), lambda t, qi, kj, *_: (0, kj[t], 0)),
            pl.BlockSpec((n_heads, tk, HEAD_W), lambda t, qi, kj, *_: (0, kj[t], 0)),
        ],
        out_specs=pl.BlockSpec((n_heads, tq, HEAD_W), lambda t, qi, kj, *_: (0, qi[t], 0)),
        scratch_shapes=[pltpu.VMEM(stat, F32), pltpu.VMEM(stat, F32), pltpu.VMEM(stat, F32)],
    )
    block_bytes = (_nbytes((n_heads, 2, tq, HEAD_W), BF16) + 2 * _nbytes((n_heads, tk, HEAD_W), BF16)
                   + _nbytes((n_heads, tq, HEAD_W), BF16))
    return pl.pallas_call(
        functools.partial(_attn_kernel, tq=tq, tk=tk, n_heads=n_heads, lam_init=lam_init),
        grid_spec=grid_spec,
        out_shape=jax.ShapeDtypeStruct((n_heads, seq, HEAD_W), BF16),
        compiler_params=pltpu.CompilerParams(
            dimension_semantics=("arbitrary",),
            vmem_limit_bytes=_vmem_limit(block_bytes, 3 * _nbytes(stat, F32))),
        name="attn_prompt",
    )(jnp.asarray(qi), jnp.asarray(kj), jnp.asarray(flags), use_bound, score_bound, lam_vec, sg, q, kb, vb)


def _attn_dec_kernel(lam_ref, sg_ref, bias_c_ref, bias_n_ref, q_ref, kn_ref, vn_ref, ck_ref, cv_ref, o_ref,
                     *, pos_chunk, t_new, n_heads, lam_init):
    lam = _lambda_value(lam_ref, lam_init)
    past = ck_ref.shape[0]
    rows = n_heads * 2 * t_new
    q = q_ref[...].reshape(rows, HEAD_W)

    def fold(carry, k_flat, v_flat, bias):
        m, l, acc = carry
        s = _dot_nt(q, k_flat) + bias
        m_new = jnp.maximum(m, jnp.max(s, axis=1, keepdims=True))
        alpha = jnp.exp(m - m_new)
        p = jnp.exp(s - m_new)
        return (m_new, alpha * l + jnp.sum(p, axis=1, keepdims=True), alpha * acc + _dot(p.astype(BF16), v_flat))

    carry = (jnp.full((rows, 1), NEG_INF, F32), jnp.zeros((rows, 1), F32), jnp.zeros((rows, HEAD_W), F32))
    for c0 in range(0, past, pos_chunk):
        k_flat = ck_ref[c0:c0 + pos_chunk].reshape(pos_chunk * n_heads, HEAD_W).astype(BF16)
        v_flat = cv_ref[c0:c0 + pos_chunk].reshape(pos_chunk * n_heads, HEAD_W).astype(BF16)
        carry = fold(carry, k_flat, v_flat, bias_c_ref[...])
    carry = fold(carry, kn_ref[...].reshape(n_heads * t_new, HEAD_W), vn_ref[...].reshape(n_heads * t_new, HEAD_W),
                 bias_n_ref[...])
    _, l, acc = carry
    o_all = acc / l
    for h in range(n_heads):
        base = h * 2 * t_new
        o = o_all[base:base + t_new] - lam * o_all[base + t_new:base + 2 * t_new]
        o_ref[h] = _sub_ln(o, sg_ref[...], lam_init).astype(BF16)


def _sample_biases(n_heads, t_new, past, pos_chunk):
    rows = n_heads * 2 * t_new
    row_head = (np.arange(rows) // (2 * t_new))[:, None]
    q_chunk = ((past + np.arange(rows) % t_new) // CHUNK)[:, None]
    assert (past - 1) // CHUNK <= past // CHUNK
    col = np.arange(pos_chunk * n_heads)[None, :]
    bias_c = np.where(col % n_heads == row_head, 0.0, NEG_INF).astype(np.float32)
    col = np.arange(n_heads * t_new)[None, :]
    visible = (col // t_new == row_head) & ((past + col % t_new) // CHUNK <= q_chunk)
    bias_n = np.where(visible, 0.0, NEG_INF).astype(np.float32)
    return jnp.asarray(bias_c), jnp.asarray(bias_n)


def _attn_sample(q, kb, vb, cache_k, cache_v, lam_vec, sg, *, layer, t_new, lam_init):
    n_heads = q.shape[0]
    _, n_seq, past, _, _ = cache_k.shape
    aw = n_heads * HEAD_W
    pos_chunk = min(past, SAMPLE_POS_CHUNK)
    assert past % pos_chunk == 0
    bias_c, bias_n = _sample_biases(n_heads, t_new, past, pos_chunk)
    cache_spec = pl.BlockSpec((None, None, past, n_heads, HEAD_W), lambda b: (layer, b, 0, 0, 0))
    block_bytes = 2 * _nbytes((past, aw), F32) + 5 * _nbytes((n_heads, t_new, HEAD_W), BF16)
    return pl.pallas_call(
        functools.partial(_attn_dec_kernel, pos_chunk=pos_chunk, t_new=t_new, n_heads=n_heads,
                          lam_init=lam_init),
        grid=(n_seq,),
        in_specs=[
            pl.BlockSpec((4, LANES), lambda b: (0, 0)),
            pl.BlockSpec((1, HEAD_W), lambda b: (0, 0)),
            pl.BlockSpec(bias_c.shape, lambda b: (0, 0), pipeline_mode=pl.Buffered(1)),
            pl.BlockSpec(bias_n.shape, lambda b: (0, 0), pipeline_mode=pl.Buffered(1)),
            pl.BlockSpec((n_heads, 2, t_new, HEAD_W), lambda b: (0, 0, b, 0)),
            pl.BlockSpec((n_heads, t_new, HEAD_W), lambda b: (0, b, 0)),
            pl.BlockSpec((n_heads, t_new, HEAD_W), lambda b: (0, b, 0)),
            cache_spec,
            cache_spec,
        ],
        out_specs=pl.BlockSpec((n_heads, t_new, HEAD_W), lambda b: (0, b, 0)),
        out_shape=jax.ShapeDtypeStruct((n_heads, n_seq * t_new, HEAD_W), BF16),
        compiler_params=pltpu.CompilerParams(
            dimension_semantics=("arbitrary",),
            vmem_limit_bytes=_vmem_limit(block_bytes, _nbytes(bias_c.shape, F32) + _nbytes(bias_n.shape, F32))),
        name="attn_sample",
    )(lam_vec, sg, bias_c, bias_n, q, kb, vb, cache_k, cache_v)


def _pool_kernel(u_ref, hist_ref, wp_ref, ps_ref, pb_ref, ext_scr, *, tm, start):
    ti = pl.program_id(1)

    @pl.when(ti == 0)
    def _():
        ext_scr[0:HIST_ROWS] = hist_ref[...]

    @pl.when(ti > 0)
    def _():
        ext_scr[0:HIST_ROWS] = ext_scr[tm:tm + HIST_ROWS]

    ext_scr[HIST_ROWS:HIST_ROWS + tm] = u_ref[...]
    pos = start + ti * tm + lax.broadcasted_iota(jnp.int32, (tm, 1), 0)
    gw = wp_ref.shape[1]
    for g, w in enumerate(POOL_WINDOWS):
        cs = slice(g * gw, (g + 1) * gw)
        ext = ext_scr[:, cs]
        win, span = ext, 1
        while span < w:
            win = win + pltpu.roll(win, span, axis=0)
            span *= 2
        cur = ext[HIST_ROWS:]
        cnt = jnp.minimum(w, pos + 1).astype(F32)
        p = win[HIST_ROWS:] / cnt - cur
        y = _dot(p.astype(BF16), wp_ref[g])
        pb_ref[:, cs] = (y * ps_ref[:, cs]).astype(BF16)


def _pool(u, hist, wp_bf, ps, *, n_seq, seq_len, tm, start):
    m, pw = u.shape
    tiles = seq_len // tm
    n_groups, gw, _ = wp_bf.shape
    block_bytes = (_nbytes((tm, pw), F32) + _nbytes((HIST_ROWS, pw), F32) + _nbytes(wp_bf.shape, BF16)
                   + _nbytes((tm, pw), BF16))
    return pl.pallas_call(
        functools.partial(_pool_kernel, tm=tm, start=start),
        grid=(n_seq, tiles),
        in_specs=[
            pl.BlockSpec((tm, pw), lambda b, t: (b * tiles + t, 0)),
            pl.BlockSpec((HIST_ROWS, pw), lambda b, t: (b, 0)),
            pl.BlockSpec((n_groups, gw, gw), lambda b, t: (0, 0, 0)),
            pl.BlockSpec((1, pw), lambda b, t: (0, 0)),
        ],
        out_specs=pl.BlockSpec((tm, pw), lambda b, t: (b * tiles + t, 0)),
        out_shape=jax.ShapeDtypeStruct((m, pw), BF16),
        scratch_shapes=[pltpu.VMEM((HIST_ROWS + tm, pw), F32)],
        compiler_params=pltpu.CompilerParams(
            dimension_semantics=("arbitrary", "arbitrary"),
            vmem_limit_bytes=_vmem_limit(block_bytes, _nbytes((HIST_ROWS + tm, pw), F32))),
        name="pool_mix",
    )(u, hist, wp_bf, ps)


def _merge_kernel(o_ref, pb_ref, sig_ref, x_ref, wua_ref, wup_ref, wout_ref, g2_ref, x1_ref, h2_ref, m_scr,
                  *, n_heads):
    d = x_ref.shape[1]
    o = jnp.concatenate([o_ref[h] for h in range(n_heads)], axis=1)
    pb = pb_ref[...]
    for n in range(d // FF_TILE):
        cs = slice(n * FF_TILE, (n + 1) * FF_TILE)
        gs = slice(d + n * FF_TILE, d + (n + 1) * FF_TILE)
        a = _dot(o, wua_ref[:, cs])
        b = _dot(pb, wup_ref[:, cs])
        m_scr[:, cs] = (sig_ref[:, cs].astype(F32) * a + sig_ref[:, gs].astype(F32) * b).astype(BF16)
    for n in range(d // FF_TILE):
        cs = slice(n * FF_TILE, (n + 1) * FF_TILE)
        x1_ref[:, cs] = x_ref[:, cs] + _dot(m_scr[...], wout_ref[:, cs])
    h2_ref[...] = _rms_rows(x1_ref[...], g2_ref[...]).astype(BF16)


def _merge(o, pb, sig, x, wua_bf, wup_bf, wout_bf, ffn_g, *, tm):
    n_heads = o.shape[0]
    m, d = x.shape
    aw = n_heads * HEAD_W
    block_bytes = (2 * _nbytes((tm, aw), BF16) + _nbytes((tm, 2 * d), BF16) + 2 * _nbytes((tm, d), F32)
                   + _nbytes((tm, d), BF16))
    weight_bytes = 2 * _nbytes((aw, d), BF16) + _nbytes((d, d), BF16)
    resident = pl.Buffered(1)
    return pl.pallas_call(
        functools.partial(_merge_kernel, n_heads=n_heads),
        grid=(m // tm,),
        in_specs=[
            pl.BlockSpec((n_heads, tm, HEAD_W), lambda i: (0, i, 0)),
            pl.BlockSpec((tm, aw), lambda i: (i, 0)),
            pl.BlockSpec((tm, 2 * d), lambda i: (i, 0)),
            pl.BlockSpec((tm, d), lambda i: (i, 0)),
            pl.BlockSpec((aw, d), lambda i: (0, 0), pipeline_mode=resident),
            pl.BlockSpec((aw, d), lambda i: (0, 0), pipeline_mode=resident),
            pl.BlockSpec((d, d), lambda i: (0, 0), pipeline_mode=resident),
            pl.BlockSpec((1, d), lambda i: (0, 0)),
        ],
        out_specs=(pl.BlockSpec((tm, d), lambda i: (i, 0)), pl.BlockSpec((tm, d), lambda i: (i, 0))),
        out_shape=(jax.ShapeDtypeStruct((m, d), F32), jax.ShapeDtypeStruct((m, d), BF16)),
        scratch_shapes=[pltpu.VMEM((tm, d), BF16)],
        compiler_params=pltpu.CompilerParams(
            dimension_semantics=("arbitrary",),
            vmem_limit_bytes=_vmem_limit(block_bytes, weight_bytes + _nbytes((tm, d), BF16))),
        name="merge_out_proj",
    )(o, pb, sig, x, wua_bf, wup_bf, wout_bf, ffn_g)


def _ffn_up_kernel(*refs, tm, seq_len, streaming):
    if streaming:
        h_ref, hp_ref, w_ref, cw_ref, cb_ref, act_ref, zt_ref, hext_scr = refs
        hext_scr[0:HIST_ROWS] = hp_ref[...]
        hext_scr[HIST_ROWS:HIST_ROWS + tm] = h_ref[...]
        lhs_ref = hext_scr
        not_first = pl.program_id(0) > 0
    else:
        h_ref, zs1_ref, zs2_ref, w_ref, cw_ref, cb_ref, act_ref, zt_ref = refs
        lhs_ref = h_ref
        r = lax.broadcasted_iota(jnp.int32, (tm, 1), 0) % seq_len
    for n in range(w_ref.shape[1] // (2 * MXU_TILE)):
        cs = slice(n * MXU_TILE, (n + 1) * MXU_TILE)
        zv = _dot(lhs_ref[...], w_ref[:, 2 * n * MXU_TILE:(2 * n + 2) * MXU_TILE])
        if streaming:
            row = lax.broadcasted_iota(jnp.int32, (HIST_ROWS + tm, 1), 0)
            zx = jnp.where((row >= HIST_ROWS) | not_first, zv[:, :MXU_TILE], 0.0)
            z = zx[HIST_ROWS:]
            val = zv[HIST_ROWS:, MXU_TILE:]
            z1 = pltpu.roll(zx, 1, axis=0)[HIST_ROWS:]
            z2 = pltpu.roll(zx, 2, axis=0)[HIST_ROWS:]
            zt_ref[:, cs] = z[tm - 8:tm]
        else:
            z = zv[:, :MXU_TILE]
            val = zv[:, MXU_TILE:]
            z1 = jnp.where(r >= 1, pltpu.roll(z, 1, axis=0), 0.0) + zs1_ref[:, cs]
            z2 = jnp.where(r >= 2, pltpu.roll(z, 2, axis=0), 0.0) + zs2_ref[:, cs]
            zt_ref[:, cs] = z
        zc = cb_ref[:, cs] + z2 * cw_ref[0:1, cs] + z1 * cw_ref[1:2, cs] + z * cw_ref[2:3, cs]
        act_ref[:, cs] = (jax.nn.silu(zc) * val).astype(BF16)


def _ffn_down_kernel(x_ref, a0_ref, a1_ref, w0_ref, w1_ref, y_ref):
    y_ref[...] = x_ref[...] + _dot(a0_ref[...], w0_ref[...]) + _dot(a1_ref[...], w1_ref[...])


def _ffn_up(h2, wcat_bf, cw, cb, *, half, tm, seq_len, zs1=None, zs2=None):
    m, d = h2.shape
    fh = wcat_bf.shape[1] // 4
    streaming = zs1 is None
    w_specs = [
        pl.BlockSpec((d, 2 * fh), lambda i: (0, half), pipeline_mode=pl.Buffered(1)),
        pl.BlockSpec((CONV_WIDTH, fh), lambda i: (0, half)),
        pl.BlockSpec((1, fh), lambda i: (0, half)),
    ]
    scratch = []
    scratch_bytes = _nbytes((d, 2 * fh), BF16)
    if streaming:
        assert seq_len == m and tm % HIST_ROWS == 0
        per_tile = tm // HIST_ROWS
        in_specs = [pl.BlockSpec((tm, d), lambda i: (i, 0)),
                    pl.BlockSpec((HIST_ROWS, d), lambda i: (jnp.maximum(i * per_tile - 1, 0), 0))] + w_specs
        args = (h2, h2, wcat_bf, cw, cb)
        zt_shape, zt_spec = (m // tm * 8, fh), pl.BlockSpec((8, fh), lambda i: (i, 0))
        block_bytes = _nbytes((tm, d), BF16) + _nbytes((tm, fh), BF16)
        scratch.append(pltpu.VMEM((HIST_ROWS + tm, d), BF16))
        scratch_bytes += _nbytes((HIST_ROWS + tm, d), BF16)
    else:
        assert tm % seq_len == 0 and seq_len >= CONV_WIDTH - 1
        in_specs = [pl.BlockSpec((tm, d), lambda i: (i, 0)),
                    pl.BlockSpec((tm, fh), lambda i: (i, half)),
                    pl.BlockSpec((tm, fh), lambda i: (i, half))] + w_specs
        args = (h2, zs1, zs2, wcat_bf, cw, cb)
        zt_shape, zt_spec = (m, fh), pl.BlockSpec((tm, fh), lambda i: (i, 0))
        block_bytes = _nbytes((tm, d), BF16) + _nbytes((tm, fh), BF16) + 3 * _nbytes((tm, fh), F32)
    return pl.pallas_call(
        functools.partial(_ffn_up_kernel, tm=tm, seq_len=seq_len, streaming=streaming),
        grid=(m // tm,),
        in_specs=in_specs,
        out_specs=(pl.BlockSpec((tm, fh), lambda i: (i, 0)), zt_spec),
        out_shape=(jax.ShapeDtypeStruct((m, fh), BF16), jax.ShapeDtypeStruct(zt_shape, F32)),
        scratch_shapes=scratch,
        compiler_params=pltpu.CompilerParams(
            dimension_semantics=("arbitrary",),
            vmem_limit_bytes=_vmem_limit(block_bytes, scratch_bytes)),
        name="conv_glu_up",
    )(*args)


def _ffn_down(x1, act0, act1, wo_bf, *, tm):
    m, d = x1.shape
    fh = act0.shape[1]
    resident = pl.Buffered(1)
    block_bytes = 2 * _nbytes((tm, d), F32) + 2 * _nbytes((tm, fh), BF16)
    return pl.pallas_call(
        _ffn_down_kernel,
        grid=(m // tm,),
        in_specs=[
            pl.BlockSpec((tm, d), lambda i: (i, 0)),
            pl.BlockSpec((tm, fh), lambda i: (i, 0)),
            pl.BlockSpec((tm, fh), lambda i: (i, 0)),
            pl.BlockSpec((fh, d), lambda i: (0, 0), pipeline_mode=resident),
            pl.BlockSpec((fh, d), lambda i: (1, 0), pipeline_mode=resident),
        ],
        out_specs=pl.BlockSpec((tm, d), lambda i: (i, 0)),
        out_shape=jax.ShapeDtypeStruct((m, d), F32),
        compiler_params=pltpu.CompilerParams(
            dimension_semantics=("arbitrary",),
            vmem_limit_bytes=_vmem_limit(block_bytes, 2 * _nbytes((fh, d), BF16))),
        name="conv_glu_down",
    )(x1, act0, act1, wo_bf, wo_bf)


def _ffn(x1, h2, wcat_bf, cw, cb, wo_bf, *, tm, seq_len, zs1=None, zs2=None):
    ups = [_ffn_up(h2, wcat_bf, cw, cb, half=half, tm=tm, seq_len=seq_len, zs1=zs1, zs2=zs2)
           for half in range(2)]
    y = _ffn_down(x1, ups[0][0], ups[1][0], wo_bf, tm=tm)
    return y, jnp.concatenate([ups[0][1], ups[1][1]], axis=1)


def _prep_weights(attn_norm_g, w_in, q_norm_g, k_norm_g, lambda_q1, lambda_k1, lambda_q2, lambda_k2,
                  subln_g, w_pool, pool_scale, w_up_attn, w_up_pool, w_out,
                  ffn_norm_g, w_ffn_in, conv_w, conv_b, w_ffn_out, *, n_heads):
    d_ff = conv_w.shape[1]
    fp = -(-d_ff // FF_TILE) * FF_TILE
    pad = fp - d_ff
    lam_vec = jnp.pad(jnp.stack([lambda_q1, lambda_k1, lambda_q2, lambda_k2]).astype(F32),
                      ((0, 0), (0, LANES - lambda_q1.shape[0])))
    group = np.arange(MXU_TILE) // QK_HALF
    gd = jnp.asarray((group[:, None] == group[None, :]).astype(np.float32) / QK_HALF, BF16)
    score_bound = (QK_HALF ** 0.5 * BOUND_MARGIN * jnp.max(jnp.abs(q_norm_g), axis=1)
                   * jnp.max(jnp.abs(k_norm_g), axis=1)).astype(F32)
    use_bound = (jnp.max(score_bound) <= MAX_FIXED_SHIFT).astype(jnp.int32).reshape(1)
    w_ffn_cat = _interleave_ffn_in(w_ffn_in, d_ff=d_ff, fp=fp, rows=CAST_ROWS_WIDE)
    return dict(
        score_bound=score_bound,
        use_bound=use_bound,
        attn_g=attn_norm_g.reshape(1, -1),
        w_in=_to_bf16(w_in, rows=CAST_ROWS_WIDE),
        qg=(jnp.tile(q_norm_g.reshape(-1), n_heads) * (QK_HALF ** -0.5)).reshape(1, -1),
        kg=jnp.tile(k_norm_g.reshape(-1), n_heads).reshape(1, -1),
        gd=gd,
        lam_vec=lam_vec,
        sg=subln_g.reshape(1, -1),
        w_pool=w_pool.astype(BF16),
        pool_scale=pool_scale.reshape(1, -1),
        w_up_attn=_to_bf16(w_up_attn, rows=CAST_ROWS),
        w_up_pool=_to_bf16(w_up_pool, rows=CAST_ROWS),
        w_out=_to_bf16(w_out, rows=CAST_ROWS),
        ffn_g=ffn_norm_g.reshape(1, -1),
        w_ffn_cat=w_ffn_cat,
        conv_w=jnp.pad(conv_w, ((0, 0), (0, pad))),
        conv_b=jnp.pad(conv_b, (0, pad)).reshape(1, -1),
        w_ffn_out=_pad_rows_bf16(w_ffn_out, fp=fp, rows=LANES),
        d_ff=d_ff,
    )


def _layer(x, start, caches, layer, pool_hist, conv_hist, lam_init, w, *, n_heads, tm, tq):
    n_seq, t_len, d = x.shape
    m = n_seq * t_len
    d_ff = w["d_ff"]
    x2 = x.reshape(m, d)

    q, k_f32, kb, v_f32, vb, u, sig = _in_proj(
        x2, w["attn_g"], w["w_in"], w["qg"], w["kg"], w["gd"], tm=min(tm, INPROJ_ROWS), n_heads=n_heads)

    if caches is None:
        o = _attn_prompt(q, kb, vb, w["lam_vec"], w["sg"], w["score_bound"], w["use_bound"],
                         tq=tq, tk=tq, lam_init=lam_init)
    else:
        o = _attn_sample(q, kb, vb, caches[0], caches[1], w["lam_vec"], w["sg"],
                         layer=layer, t_new=t_len, lam_init=lam_init)

    hist = jnp.pad(pool_hist, ((0, 0), (HIST_ROWS - POOL_HIST, 0), (0, 0))).reshape(n_seq * HIST_ROWS, -1)
    pool_tm = min(tm, t_len)
    pb = _pool(u, hist, w["w_pool"], w["pool_scale"], n_seq=n_seq, seq_len=t_len, tm=pool_tm, start=start)

    x1, h2 = _merge(o, pb, sig, x2, w["w_up_attn"], w["w_up_pool"], w["w_out"], w["ffn_g"], tm=tm)

    ffn_tm = min(tm, FFN_ROWS)
    if conv_hist is None:
        y, z_tail = _ffn(x1, h2, w["w_ffn_cat"], w["conv_w"], w["conv_b"], w["w_ffn_out"],
                         tm=ffn_tm, seq_len=t_len)
        conv_state = z_tail[-(CONV_WIDTH - 1):, :d_ff][None]
    else:
        fpad = w["conv_b"].shape[1] - d_ff
        zs1 = jnp.pad(conv_hist[:, 1:2], ((0, 0), (0, t_len - 1), (0, fpad))).reshape(m, -1)
        zs2 = jnp.pad(conv_hist, ((0, 0), (0, t_len - (CONV_WIDTH - 1)), (0, fpad))).reshape(m, -1)
        y, z_all = _ffn(x1, h2, w["w_ffn_cat"], w["conv_w"], w["conv_b"], w["w_ffn_out"],
                        tm=ffn_tm, seq_len=t_len, zs1=zs1, zs2=zs2)
        conv_state = z_all.reshape(n_seq, t_len, -1)[:, -(CONV_WIDTH - 1):, :d_ff]

    u3 = u.reshape(n_seq, t_len, -1)
    pool_state = jnp.concatenate([pool_hist, u3], axis=1)[:, -POOL_HIST:] if t_len < POOL_HIST else u3[:, -POOL_HIST:]
    return (y.reshape(n_seq, t_len, d),
            k_f32.reshape(n_seq, t_len, n_heads, HEAD_W),
            v_f32.reshape(n_seq, t_len, n_heads, HEAD_W),
            pool_state, conv_state)


def kernel(x_prompt, x_sample, cache_k, cache_v, state_pool, state_conv, attn_norm_g, w_in, q_norm_g, k_norm_g, lambda_q1, lambda_k1, lambda_q2, lambda_k2, subln_g, w_pool, pool_scale, w_up_attn, w_up_pool, w_out, ffn_norm_g, w_ffn_in, conv_w, conv_b, w_ffn_out):
    depth = w_in.shape[0]
    n_heads = cache_k.shape[3]
    past = cache_k.shape[2]
    y_p, y_s = x_prompt, x_sample
    outs = [[] for _ in range(8)]
    for l in range(depth):
        lam_init = 0.8 - 0.6 * math.exp(-0.3 * l)
        w = _prep_weights(attn_norm_g[l], w_in[l], q_norm_g[l], k_norm_g[l], lambda_q1[l], lambda_k1[l],
                          lambda_q2[l], lambda_k2[l], subln_g[l], w_pool[l], pool_scale[l], w_up_attn[l],
                          w_up_pool[l], w_out[l], ffn_norm_g[l], w_ffn_in[l], conv_w[l], conv_b[l],
                          w_ffn_out[l], n_heads=n_heads)
        pool0 = jnp.zeros((x_prompt.shape[0], POOL_HIST, state_pool.shape[-1]), F32)
        y_p, kp, vp, pp, cp = _layer(y_p, 0, None, l, pool0, None, lam_init, w,
                                     n_heads=n_heads, tm=512, tq=512)
        y_s, ks, vs, ps, cs = _layer(y_s, past, (cache_k, cache_v), l, state_pool[l], state_conv[l],
                                     lam_init, w, n_heads=n_heads,
                                     tm=x_sample.shape[0] * x_sample.shape[1], tq=None)
        for lst, val in zip(outs, (kp, vp, pp, cp, ks, vs, ps, cs)):
            lst.append(val)
    return (y_p, y_s) + tuple(jnp.stack(lst) for lst in outs)
```
